```python
import math
import jax
import jax.numpy as jnp
from jax import lax
import numpy as np


D_MODEL = 4096
BATCH = 4
SEQ = 2048
DEPTH = 2
DEC_BATCH = 8
DEC_SEQ = 2048
PAST_LEN = 128

N_META = 16
CHUNK = 64
PAD = CHUNK - N_META
Q_BLOCK = 128
EPS = 1e-6
NEG = -1e30
F32 = jnp.float32
MIX_W = D_MODEL // 2
A_HEADS = 4
A_DV = MIX_W // A_HEADS
A_DQK = A_DV // 2
B_HEADS = 16
B_DK = MIX_W // B_HEADS
B_DV = B_DK
SHORT_CONV = 3
C_HEADS = 16
C_DK = 128
C_DV = MIX_W // C_HEADS
D_HEADS = 8
D_HD = MIX_W // (2 * D_HEADS)
FFN_CONV = 3
D_FF = 11008
N_EVEN = (DEPTH + 1) // 2
N_ODD = DEPTH // 2
AB_SIZES = (A_HEADS * A_DQK, A_HEADS * A_DQK, A_HEADS * A_DV, A_HEADS * A_DV, 4 * A_HEADS, 3 * MIX_W, MIX_W, 4 * B_HEADS)
AB_COLS = sum(AB_SIZES)
CD_SIZES = (C_HEADS * C_DK, 2 * C_HEADS * C_DK, C_HEADS * C_DV, C_HEADS * C_DV, D_HEADS * 2 * D_HD, D_HEADS * 2 * D_HD, D_HEADS * 2 * D_HD)
CD_COLS = sum(CD_SIZES)

kernel_name = "hybrid_bidir_mlstm_gdn_hgrn2_diffattn"


def rmsnorm(x, g):
    xf = x.astype(F32)
    y = xf * lax.rsqrt(jnp.mean(xf * xf, axis=-1, keepdims=True) + EPS) * g.astype(F32)
    return y.astype(x.dtype)


def head_rmsnorm(x, n_heads, g):
    B, L, W = x.shape
    d = W // n_heads
    xh = x.astype(F32).reshape(B, L, n_heads, d)
    y = xh * lax.rsqrt(jnp.mean(xh * xh, axis=-1, keepdims=True) + EPS) * g.astype(F32).reshape(-1, d)
    return y.reshape(B, L, W)


def l2norm(x):
    return x * lax.rsqrt(jnp.sum(x * x, axis=-1, keepdims=True) + EPS)


def split_cols(x, sizes):
    return jnp.split(x, np.cumsum(sizes)[:-1].tolist(), axis=-1)


def dwconv_centred(x, w):
    K, C = w.shape
    return lax.conv_general_dilated(x, w[:, None, :].astype(x.dtype), window_strides=(1,), padding=[(K // 2, K // 2)], dimension_numbers=("NWC", "WIO", "NWC"), feature_group_count=C)


def pad_front(x, value=0.0):
    return jnp.pad(x, ((0, 0), (PAD, 0)) + ((0, 0),) * (x.ndim - 2), constant_values=value)


def to_heads(x, n_heads):
    B, T, W = x.shape
    return x.reshape(B, T, n_heads, W // n_heads).transpose(0, 2, 1, 3)


def from_heads(x):
    B, H, T, d = x.shape
    return x.transpose(0, 2, 1, 3).reshape(B, T, H * d)


def rev(a):
    return jnp.flip(a, axis=2)


def to_chunks(a):
    B, H, T = a.shape[:3]
    return jnp.moveaxis(a.reshape(B, H, T // CHUNK, CHUNK, *a.shape[3:]), 2, 0)


def from_chunks(y):
    N, B, H, C, d = y.shape
    return jnp.moveaxis(y, 0, 2).reshape(B, H, N * C, d)


def bidirectional(fn, shared, gates_fwd, gates_bwd):
    y_f = fn(*shared, *gates_fwd)
    y_b = fn(*[rev(a) for a in shared], *[rev(a) for a in gates_bwd])
    return y_f + rev(y_b)


def mlstm_chunked(q, k, v, i_pre, logf):
    B, H, T, dk = q.shape
    dv = v.shape[-1]
    causal = jnp.tril(jnp.ones((CHUNK, CHUNK), dtype=bool))

    def step(carry, inp):
        c_mat, n_vec, m = carry
        qj, kj, vj, ij, fj = inp
        b = jnp.cumsum(fj, axis=-1)
        dmat = jnp.where(causal, b[..., :, None] - b[..., None, :] + ij[..., None, :], NEG)
        inter = b + m[..., None]
        m_t = jnp.maximum(inter, jnp.max(dmat, axis=-1))
        w = jnp.exp(dmat - m_t[..., None]) * jnp.einsum("bhtd,bhsd->bhts", qj, kj)
        s_inter = jnp.exp(inter - m_t)
        num = s_inter[..., None] * jnp.einsum("bhvd,bhtd->bhtv", c_mat, qj) + jnp.einsum("bhts,bhsv->bhtv", w, vj)
        den = s_inter * jnp.einsum("bhd,bhtd->bht", n_vec, qj) + jnp.sum(w, axis=-1)
        h = num / jnp.maximum(jnp.abs(den), jnp.exp(-m_t))[..., None]
        b_last = b[..., -1]
        dec = b_last[..., None] - b + ij
        m_new = jnp.maximum(b_last + m, jnp.max(dec, axis=-1))
        a_old = jnp.exp(b_last + m - m_new)
        w_in = jnp.exp(dec - m_new[..., None])
        c_mat = a_old[..., None, None] * c_mat + jnp.einsum("bhs,bhsv,bhsd->bhvd", w_in, vj, kj)
        n_vec = a_old[..., None] * n_vec + jnp.einsum("bhs,bhsd->bhd", w_in, kj)
        return (c_mat, n_vec, m_new), h

    init = (jnp.zeros((B, H, dv, dk), F32), jnp.zeros((B, H, dk), F32), jnp.full((B, H), NEG, F32))
    _, hs = lax.scan(step, init, tuple(to_chunks(a) for a in (q, k, v, i_pre, logf)))
    return from_chunks(hs)


def gated_delta_chunked(q, k, v, g, beta):
    B, H, T, dk = q.shape
    dv = v.shape[-1]
    incl = jnp.tril(jnp.ones((CHUNK, CHUNK), dtype=bool))
    strict = jnp.tril(jnp.ones((CHUNK, CHUNK), dtype=bool), -1)

    def step(S, inp):
        qj, kj, vj, gj, bj = inp
        b = jnp.cumsum(gj, axis=-1)
        decay = jnp.exp(jnp.where(incl, b[..., :, None] - b[..., None, :], -jnp.inf))
        kb = kj * bj[..., None]
        a_mat = jnp.where(strict, jnp.einsum("bhtd,bhsd->bhts", kb, kj) * decay, 0.0)
        rhs = jnp.concatenate([vj * bj[..., None], kb * jnp.exp(b)[..., None]], axis=-1)
        sol = lax.linalg.triangular_solve(a_mat, rhs, left_side=True, lower=True, unit_diagonal=True)
        u, w = sol[..., :dv], sol[..., dv:]
        v_new = u - jnp.einsum("bhtd,bhdv->bhtv", w, S)
        attn = jnp.einsum("bhtd,bhsd->bhts", qj, kj) * decay
        o = jnp.einsum("bhtd,bhdv->bhtv", qj * jnp.exp(b)[..., None], S) + jnp.einsum("bhts,bhsv->bhtv", attn, v_new)
        b_last = b[..., -1:]
        S = S * jnp.exp(b_last)[..., None] + jnp.einsum("bhsd,bhsv->bhdv", kj * jnp.exp(b_last - b)[..., None], v_new)
        return S, o

    init = jnp.zeros((B, H, dk, dv), F32)
    _, os_ = lax.scan(step, init, tuple(to_chunks(a) for a in (q, k, v, g, beta)))
    return from_chunks(os_)


def gla_chunked(q, v, k, g):
    B, H, T, dk = q.shape
    dv = v.shape[-1]
    incl = jnp.tril(jnp.ones((CHUNK, CHUNK), dtype=bool))

    def step(S, inp):
        qj, vj, kj, gj = inp
        b = jnp.cumsum(gj, axis=-2)
        rel = jnp.where(incl[:, :, None], b[..., :, None, :] - b[..., None, :, :], -jnp.inf)
        attn = jnp.sum(qj[..., :, None, :] * kj[..., None, :, :] * jnp.exp(rel), axis=-1)
        o = jnp.einsum("bhtd,bhdv->bhtv", qj * jnp.exp(b), S) + jnp.einsum("bhts,bhsv->bhtv", attn, vj)
        b_last = b[..., -1:, :]
        S = jnp.exp(b_last[..., 0, :])[..., None] * S + jnp.einsum("bhsd,bhsv->bhdv", kj * jnp.exp(b_last - b), vj)
        return S, o

    init = jnp.zeros((B, H, dk, dv), F32)
    _, os_ = lax.scan(step, init, tuple(to_chunks(a) for a in (q, v, k, g)))
    return from_chunks(os_)


def diff_attention(q, k, v, lam):
    B, L = q.shape[:2]
    S = L - N_META
    pos = jnp.arange(L, dtype=F32)
    slopes = 2.0 ** (-8.0 * jnp.arange(1, D_HEADS + 1, dtype=F32) / D_HEADS)
    vf = v.astype(F32)
    scale = D_HD ** -0.5

    def block(qb, qpos):
        s = jnp.einsum("bqhmd,bkhmd->bhmqk", qb, k).astype(F32) * scale
        s = s - slopes[:, None, None, None] * jnp.abs(qpos[:, None] - pos[None, :])
        p = jax.nn.softmax(s, axis=-1)
        a = p[:, :, 0] - lam * p[:, :, 1]
        return jnp.einsum("bhqk,bkhd->bqhd", a, vf)

    meta_out = block(q[:, :N_META], pos[:N_META])
    nb = S // Q_BLOCK
    qr = jnp.moveaxis(q[:, N_META:].reshape(B, nb, Q_BLOCK, *q.shape[2:]), 1, 0)
    pr = pos[N_META:].reshape(nb, Q_BLOCK)
    real = lax.map(lambda a: block(a[0], a[1]), (qr, pr))
    real = jnp.moveaxis(real, 0, 1).reshape(B, S, *real.shape[3:])
    return jnp.concatenate([meta_out, real], axis=1)


def mixer_ab(h, w_in, w_out, i_bias, f_bias, m_norm, conv_w, a_log, dt_bias, dn_norm):
    B, L, _ = h.shape
    aq, ak, av, ao, ag, bqkv, bz, bg = split_cols(h @ w_in, AB_SIZES)
    qa = to_heads(pad_front(aq.astype(F32)), A_HEADS)
    ka = to_heads(pad_front(ak.astype(F32)), A_HEADS) * (A_DQK ** -0.5)
    va = to_heads(pad_front(av.astype(F32)), A_HEADS)
    ag = ag.astype(F32).reshape(B, L, 2, 2, A_HEADS)
    i_bias = i_bias.astype(F32)
    f_bias = f_bias.astype(F32)

    def mlstm_gates(d):
        i_pre = pad_front(ag[:, :, d, 0] + i_bias[d], NEG).transpose(0, 2, 1)
        logf = pad_front(jax.nn.log_sigmoid(ag[:, :, d, 1] + f_bias[d])).transpose(0, 2, 1)
        return (i_pre, logf)

    ha = bidirectional(mlstm_chunked, (qa, ka, va), mlstm_gates(0), mlstm_gates(1))[:, :, PAD:]
    ha = head_rmsnorm(from_heads(ha), A_HEADS, m_norm) * jax.nn.sigmoid(ao.astype(F32))
    qkv = jax.nn.silu(dwconv_centred(bqkv, conv_w))
    bq, bk, bv = jnp.split(qkv, 3, axis=-1)
    qb = l2norm(to_heads(pad_front(bq.astype(F32)), B_HEADS)) * (B_DK ** -0.5)
    kb = l2norm(to_heads(pad_front(bk.astype(F32)), B_HEADS))
    vb = to_heads(pad_front(bv.astype(F32)), B_HEADS)
    bg = bg.astype(F32).reshape(B, L, 2, 2, B_HEADS)

    def dn_gates(d):
        g = -jnp.exp(a_log[d].astype(F32)) * jax.nn.softplus(bg[:, :, d, 0] + dt_bias[d].astype(F32))
        beta = jax.nn.sigmoid(bg[:, :, d, 1])
        return (pad_front(g).transpose(0, 2, 1), pad_front(beta).transpose(0, 2, 1))

    hb = bidirectional(gated_delta_chunked, (qb, kb, vb), dn_gates(0), dn_gates(1))[:, :, PAD:]
    hb = head_rmsnorm(from_heads(hb), B_HEADS, dn_norm) * jax.nn.silu(bz.astype(F32))
    return jnp.concatenate([ha, hb], axis=-1).astype(h.dtype) @ w_out


def mixer_cd(h, layer, w_in, w_out, lower_bounds, hg_norm, q_norm, k_norm, lam_vec, sub_norm):
    B, L, _ = h.shape
    cq, cf, cv, cg, dq, dk, dv = split_cols(h @ w_in, CD_SIZES)
    p = jax.nn.softmax(lower_bounds.astype(F32), axis=0)
    lb = (jnp.cumsum(p, axis=0) - p[0])[layer]
    qc = to_heads(pad_front(jax.nn.silu(cq.astype(F32))), C_HEADS)
    vc = to_heads(pad_front(cv.astype(F32)), C_HEADS)
    cf = cf.astype(F32).reshape(B, L, 2, C_HEADS * C_DK)

    def hgrn_gates(d):
        logf = jnp.logaddexp(jnp.log(lb), jnp.log1p(-lb) + jax.nn.log_sigmoid(cf[:, :, d]))
        key = -jnp.expm1(logf)
        return (to_heads(pad_front(key), C_HEADS), to_heads(pad_front(logf), C_HEADS))

    hc = bidirectional(gla_chunked, (qc, vc), hgrn_gates(0), hgrn_gates(1))[:, :, PAD:]
    hc = head_rmsnorm(from_heads(hc), C_HEADS, hg_norm) * jax.nn.silu(cg.astype(F32))
    qd = rmsnorm(dq.reshape(B, L, D_HEADS, 2, D_HD), q_norm)
    kd = rmsnorm(dk.reshape(B, L, D_HEADS, 2, D_HD), k_norm)
    vd = dv.reshape(B, L, D_HEADS, 2 * D_HD)
    lam_init = 0.8 - 0.6 * math.exp(-0.3 * layer)
    lv = lam_vec.astype(F32)
    lam = jnp.exp(jnp.sum(lv[0] * lv[1])) - jnp.exp(jnp.sum(lv[2] * lv[3])) + lam_init
    od = rmsnorm(diff_attention(qd, kd, vd, lam), sub_norm) * (1.0 - lam_init)
    hd = od.reshape(B, L, D_HEADS * 2 * D_HD)
    return jnp.concatenate([hc, hd], axis=-1).astype(h.dtype) @ w_out


def conv_ffn(h, w_up, conv_w, conv_b, w_down):
    gate, val = jnp.split(h @ w_up, 2, axis=-1)
    gate = dwconv_centred(gate, conv_w) + conv_b.astype(gate.dtype)
    return (jax.nn.silu(gate) * val) @ w_down


def trunk(x, meta_tokens, norm_mix, norm_ffn, ab_w_in, ab_w_out, mlstm_i_bias, mlstm_f_bias, mlstm_norm, dn_conv, dn_a_log, dn_dt_bias, dn_norm, cd_w_in, cd_w_out, hgrn_lower_bounds, hgrn_norm, diff_q_norm, diff_k_norm, diff_lambda, diff_norm, ffn_w_up, ffn_conv_w, ffn_conv_b, ffn_w_down):
    B = x.shape[0]
    meta = jnp.broadcast_to(meta_tokens.astype(x.dtype)[None], (B, N_META, D_MODEL))
    h = jnp.concatenate([meta, x], axis=1)
    for l in range(DEPTH):
        hn = rmsnorm(h, norm_mix[l])
        if l % 2 == 0:
            e = l // 2
            mix = mixer_ab(hn, ab_w_in[e], ab_w_out[e], mlstm_i_bias[e], mlstm_f_bias[e], mlstm_norm[e], dn_conv[e], dn_a_log[e], dn_dt_bias[e], dn_norm[e])
        else:
            o = l // 2
            mix = mixer_cd(hn, l, cd_w_in[o], cd_w_out[o], hgrn_lower_bounds, hgrn_norm[o], diff_q_norm[o], diff_k_norm[o], diff_lambda[o], diff_norm[o])
        h = h + mix
        h = h + conv_ffn(rmsnorm(h, norm_ffn[l]), ffn_w_up[l], ffn_conv_w[l], ffn_conv_b[l], ffn_w_down[l])
    return h[:, N_META:]


def setup_inputs(seed: int = 0) -> dict:
    key = jax.random.key(seed)
    ks = jax.random.split(key, 32)

    def nrm(k, shape, s):
        return jax.random.normal(k, shape, F32) * s

    dt = jnp.exp(jax.random.uniform(ks[10], (N_EVEN, 2, B_HEADS), F32, math.log(1e-3), math.log(1e-1)))
    return {
        "x_prompt": nrm(ks[0], (BATCH, SEQ, D_MODEL), 1.0),
        "x_sample": nrm(ks[1], (DEC_BATCH, DEC_SEQ, D_MODEL), 1.0),
        "meta_tokens": nrm(ks[2], (N_META, D_MODEL), 1.0),
        "norm_mix": 1.0 + nrm(ks[3], (DEPTH, D_MODEL), 0.02),
        "norm_ffn": 1.0 + nrm(ks[4], (DEPTH, D_MODEL), 0.02),
        "ab_w_in": nrm(ks[5], (N_EVEN, D_MODEL, AB_COLS), D_MODEL ** -0.5),
        "ab_w_out": nrm(ks[6], (N_EVEN, 2 * MIX_W, D_MODEL), (2 * MIX_W) ** -0.5),
        "mlstm_i_bias": -1.0 + nrm(ks[7], (N_EVEN, 2, A_HEADS), 0.1),
        "mlstm_f_bias": 3.0 + jax.random.uniform(ks[8], (N_EVEN, 2, A_HEADS), F32, 0.0, 3.0),
        "mlstm_norm": 1.0 + nrm(ks[9], (N_EVEN, A_HEADS * A_DV), 0.02),
        "dn_conv": nrm(ks[11], (N_EVEN, SHORT_CONV, 3 * MIX_W), 0.5),
        "dn_a_log": jnp.log(jax.random.uniform(ks[12], (N_EVEN, 2, B_HEADS), F32, 1.0, 16.0)),
        "dn_dt_bias": dt + jnp.log(-jnp.expm1(-dt)),
        "dn_norm": 1.0 + nrm(ks[13], (N_EVEN, B_DV), 0.02),
        "cd_w_in": nrm(ks[14], (N_ODD, D_MODEL, CD_COLS), D_MODEL ** -0.5),
        "cd_w_out": nrm(ks[15], (N_ODD, 2 * MIX_W, D_MODEL), (2 * MIX_W) ** -0.5),
        "hgrn_lower_bounds": nrm(ks[16], (DEPTH, C_HEADS * C_DK), 0.1),
        "hgrn_norm": 1.0 + nrm(ks[17], (N_ODD, C_DV), 0.02),
        "diff_q_norm": 1.0 + nrm(ks[18], (N_ODD, D_HD), 0.02),
        "diff_k_norm": 1.0 + nrm(ks[19], (N_ODD, D_HD), 0.02),
        "diff_lambda": nrm(ks[20], (N_ODD, 4, D_HD), 0.1),
        "diff_norm": 1.0 + nrm(ks[21], (N_ODD, 2 * D_HD), 0.02),
        "ffn_w_up": nrm(ks[22], (DEPTH, D_MODEL, 2 * D_FF), D_MODEL ** -0.5),
        "ffn_conv_w": nrm(ks[23], (DEPTH, FFN_CONV, D_FF), 0.5),
        "ffn_conv_b": nrm(ks[24], (DEPTH, D_FF), 0.02),
        "ffn_w_down": nrm(ks[25], (DEPTH, D_FF, D_MODEL), D_FF ** -0.5),
    }


def reference(x_prompt, x_sample, meta_tokens, norm_mix, norm_ffn, ab_w_in, ab_w_out, mlstm_i_bias, mlstm_f_bias, mlstm_norm, dn_conv, dn_a_log, dn_dt_bias, dn_norm, cd_w_in, cd_w_out, hgrn_lower_bounds, hgrn_norm, diff_q_norm, diff_k_norm, diff_lambda, diff_norm, ffn_w_up, ffn_conv_w, ffn_conv_b, ffn_w_down):
    weights = (meta_tokens, norm_mix, norm_ffn, ab_w_in, ab_w_out, mlstm_i_bias, mlstm_f_bias, mlstm_norm, dn_conv, dn_a_log, dn_dt_bias, dn_norm, cd_w_in, cd_w_out, hgrn_lower_bounds, hgrn_norm, diff_q_norm, diff_k_norm, diff_lambda, diff_norm, ffn_w_up, ffn_conv_w, ffn_conv_b, ffn_w_down)
    y_prompt = trunk(x_prompt, *weights)
    y_sample = trunk(x_sample, *weights)
    return (y_prompt, y_sample)
```

```python
import functools
import math

import jax
import jax.numpy as jnp
from jax import lax
from jax.experimental import pallas as pl
from jax.experimental.pallas import tpu as pltpu

F32 = jnp.float32
BF16 = jnp.bfloat16
EPS = 1e-6
NEG = -1e30
N_META = 16
CHUNK = 64
PAD = CHUNK - N_META
SUB = 16
LANE = 128
VMEM_LIMIT = 56 * 1024 * 1024


def _cp(sem, vmem=VMEM_LIMIT):
    return pltpu.CompilerParams(dimension_semantics=sem, vmem_limit_bytes=vmem)


def _iota(shape, dim):
    return lax.broadcasted_iota(jnp.int32, shape, dim)


def _dot(a, b):
    return jnp.dot(a.astype(BF16), b.astype(BF16), preferred_element_type=F32)


def _dot_nt(a, b):
    return lax.dot_general(a.astype(BF16), b.astype(BF16), (((1,), (1,)), ((), ())), preferred_element_type=F32)


def _dot_tn(a, b):
    return lax.dot_general(a.astype(BF16), b.astype(BF16), (((0,), (0,)), ((), ())), preferred_element_type=F32)


def _log_sigmoid(x):
    return jnp.minimum(x, 0.0) - jnp.log1p(jnp.exp(-jnp.abs(x)))


def _softplus(x):
    return jnp.maximum(x, 0.0) + jnp.log1p(jnp.exp(-jnp.abs(x)))


def _sigmoid(x):
    return 1.0 / (1.0 + jnp.exp(-x))


def _silu(x):
    return x * _sigmoid(x)


def _pick_col(tile, idx):
    return jnp.sum(jnp.where(_iota(tile.shape, 1) == idx, tile, 0.0), axis=1, keepdims=True)


def _to_row(col, eye):
    return jnp.sum(jnp.where(eye, col, 0.0), axis=0, keepdims=True)


def _tile(n, cands):
    for c in cands:
        if n % c == 0:
            return c
    return n


def _rms_kernel(x_ref, g_ref, o_ref):
    x = x_ref[...]
    ms = jnp.mean(x * x, axis=-1, keepdims=True)
    o_ref[...] = (x * lax.rsqrt(ms + EPS) * g_ref[...]).astype(o_ref.dtype)


def _rmsnorm_bf16(x, g):
    m, d = x.shape
    tm = _tile(m, (528, 264, 192, 64))
    return pl.pallas_call(
        _rms_kernel,
        out_shape=jax.ShapeDtypeStruct((m, d), BF16),
        grid=(m // tm,),
        in_specs=[pl.BlockSpec((tm, d), lambda i: (i, 0)), pl.BlockSpec((1, d), lambda i: (0, 0))],
        out_specs=pl.BlockSpec((tm, d), lambda i: (i, 0)),
        compiler_params=_cp(("parallel",)),
        name="rmsnorm",
    )(x, g.reshape(1, d).astype(F32))


def _mm_kernel(x_ref, w_ref, o_ref):
    o_ref[...] = jnp.dot(x_ref[...], w_ref[...], preferred_element_type=F32).astype(o_ref.dtype)


def _matmul(x, w, out_dtype=F32, name="matmul"):
    m, k = x.shape
    n = w.shape[1]
    tm = _tile(m, (1056, 704, 192, 64))
    tn = _tile(n, (512, 256, 128))
    return pl.pallas_call(
        _mm_kernel,
        out_shape=jax.ShapeDtypeStruct((m, n), out_dtype),
        grid=(m // tm, n // tn),
        in_specs=[pl.BlockSpec((tm, k), lambda i, j: (i, 0)), pl.BlockSpec((k, tn), lambda i, j: (0, j))],
        out_specs=pl.BlockSpec((tm, tn), lambda i, j: (i, j)),
        compiler_params=_cp(("parallel", "arbitrary")),
        name=name,
    )(x, w)


def _mm2_res_kernel(xa_ref, xb_ref, wa_ref, wb_ref, r_ref, o_ref):
    acc = jnp.dot(xa_ref[...], wa_ref[...], preferred_element_type=F32)
    acc = acc + jnp.dot(xb_ref[...], wb_ref[...], preferred_element_type=F32)
    o_ref[...] = r_ref[...] + acc


def _matmul2_residual(xa, xb, w, res, name="mix_out"):
    m, ka = xa.shape
    kb = xb.shape[1]
    n = w.shape[1]
    tm = _tile(m, (1056, 704, 192, 64))
    tn = _tile(n, (512, 256, 128))
    assert ka == kb
    return pl.pallas_call(
        _mm2_res_kernel,
        out_shape=jax.ShapeDtypeStruct((m, n), F32),
        grid=(m // tm, n // tn),
        in_specs=[
            pl.BlockSpec((tm, ka), lambda i, j: (i, 0)),
            pl.BlockSpec((tm, kb), lambda i, j: (i, 0)),
            pl.BlockSpec((ka, tn), lambda i, j: (0, j)),
            pl.BlockSpec((kb, tn), lambda i, j: (1, j)),
            pl.BlockSpec((tm, tn), lambda i, j: (i, j)),
        ],
        out_specs=pl.BlockSpec((tm, tn), lambda i, j: (i, j)),
        compiler_params=_cp(("parallel", "arbitrary")),
        name=name,
    )(xa, xb, w, w, res)


def _mm_res_kernel(x_ref, w_ref, r_ref, o_ref):
    o_ref[...] = r_ref[...] + jnp.dot(x_ref[...], w_ref[...], preferred_element_type=F32)


def _matmul_residual(x, w, res, name="ffn_down"):
    m, k = x.shape
    n = w.shape[1]
    tm = _tile(m, (704, 192, 64))
    tn = _tile(n, (256, 128))
    return pl.pallas_call(
        _mm_res_kernel,
        out_shape=jax.ShapeDtypeStruct((m, n), F32),
        grid=(m // tm, n // tn),
        in_specs=[
            pl.BlockSpec((tm, k), lambda i, j: (i, 0)),
            pl.BlockSpec((k, tn), lambda i, j: (0, j)),
            pl.BlockSpec((tm, tn), lambda i, j: (i, j)),
        ],
        out_specs=pl.BlockSpec((tm, tn), lambda i, j: (i, j)),
        compiler_params=_cp(("parallel", "arbitrary")),
        name=name,
    )(x, w, res)


def _conv3_rows(y, cw):
    t = y.shape[0]
    return cw[0:1] * pltpu.roll(y, 1, 0) + cw[1:2] * y + cw[2:3] * pltpu.roll(y, t - 1, 0)


def _ffn_up_kernel(x_ref, wg_ref, wv_ref, cw_ref, cb_ref, o_ref):
    x = x_ref[...]
    g = jnp.dot(x, wg_ref[...], preferred_element_type=F32)
    v = jnp.dot(x, wv_ref[...], preferred_element_type=F32)
    gc = _conv3_rows(g, cw_ref[...]) + cb_ref[...]
    o_ref[...] = (_silu(gc) * v).astype(o_ref.dtype)


def _ffn_up(xn, w_up, conv_w, conv_b, t):
    m, d = xn.shape
    f = conv_w.shape[1]
    tn = _tile(f, (256, 128))
    nj = f // tn
    return pl.pallas_call(
        _ffn_up_kernel,
        out_shape=jax.ShapeDtypeStruct((m, f), BF16),
        grid=(m // t, nj),
        in_specs=[
            pl.BlockSpec((t, d), lambda i, j: (i, 0), pipeline_mode=pl.Buffered(1)),
            pl.BlockSpec((d, tn), lambda i, j: (0, j)),
            pl.BlockSpec((d, tn), lambda i, j: (0, j + nj)),
            pl.BlockSpec((3, tn), lambda i, j: (0, j)),
            pl.BlockSpec((1, tn), lambda i, j: (0, j)),
        ],
        out_specs=pl.BlockSpec((t, tn), lambda i, j: (i, j)),
        compiler_params=_cp(("parallel", "arbitrary")),
        name="ffn_up",
    )(xn, w_up, w_up, conv_w.astype(F32), conv_b.reshape(1, f).astype(F32))


def _qkv_conv_kernel(x_ref, w_ref, cw_ref, o_ref, *, nq, qscale):
    j = pl.program_id(1)
    y = jnp.dot(x_ref[...], w_ref[...], preferred_element_type=F32)
    a = _silu(_conv3_rows(y, cw_ref[...]))
    t, tn = a.shape
    a = jnp.where(_iota((t, 1), 0) >= PAD, a, 0.0)
    is_qk = j < 2 * nq
    scale = jnp.where(j < nq, qscale, 1.0)
    for s in range(tn // LANE):
        seg = a[:, s * LANE:(s + 1) * LANE]
        ss = jnp.sum(seg * seg, axis=1, keepdims=True)
        nrm = seg * lax.rsqrt(ss + EPS) * scale
        o_ref[:, s * LANE:(s + 1) * LANE] = jnp.where(is_qk, nrm, seg)


def _qkv_conv(xn, w, conv_w, t, dk):
    m, d = xn.shape
    n = w.shape[1]
    tn = _tile(n // 3, (256, 128))
    return pl.pallas_call(
        functools.partial(_qkv_conv_kernel, nq=(n // 3) // tn, qscale=dk ** -0.5),
        out_shape=jax.ShapeDtypeStruct((m, n), F32),
        grid=(m // t, n // tn),
        in_specs=[
            pl.BlockSpec((t, d), lambda i, j: (i, 0), pipeline_mode=pl.Buffered(1)),
            pl.BlockSpec((d, tn), lambda i, j: (0, j)),
            pl.BlockSpec((3, tn), lambda i, j: (0, j)),
        ],
        out_specs=pl.BlockSpec((t, tn), lambda i, j: (i, j)),
        compiler_params=_cp(("parallel", "arbitrary")),
        name="dn_qkv_conv",
    )(xn, w, conv_w.astype(F32))


def _chunk_masks(reverse):
    r = _iota((CHUNK, CHUNK), 0)
    s = _iota((CHUNK, CHUNK), 1)
    tri = (s >= r) if reverse else (s <= r)
    tri_t = (s <= r) if reverse else (s >= r)
    strict = (s > r) if reverse else (s < r)
    return tri, tri_t, strict, r == s


def _scan_sums(f_col, tri, tri_t, eye):
    f_row = _to_row(f_col, eye)
    b_col = jnp.sum(jnp.where(tri, f_row, 0.0), axis=1, keepdims=True)
    b_row = jnp.sum(jnp.where(tri_t, f_col, 0.0), axis=0, keepdims=True)
    return b_col, b_row


def _valid_rows(chunk):
    return (chunk * CHUNK + _iota((CHUNK, 1), 0)) >= PAD


def _head_norm_gate(y, gain, gate, valid):
    ms = jnp.mean(y * y, axis=1, keepdims=True)
    return jnp.where(valid, y * lax.rsqrt(ms + EPS) * gain * gate, 0.0)


def _mlstm_kernel(*refs, heads, dk, dv, reverse):
    if reverse:
        q_ref, k_ref, v_ref, g_ref, bias_ref, yf_ref, ao_ref, nrm_ref, o_ref, c_ref, n_ref, m_ref = refs
    else:
        q_ref, k_ref, v_ref, g_ref, bias_ref, o_ref, c_ref, n_ref, m_ref = refs
    step = pl.program_id(1)
    chunk = pl.num_programs(1) - 1 - step if reverse else step

    @pl.when(step == 0)
    def _():
        c_ref[...] = jnp.zeros_like(c_ref)
        n_ref[...] = jnp.zeros_like(n_ref)
        m_ref[...] = jnp.full_like(m_ref, NEG)

    valid = _valid_rows(chunk)
    tri, tri_t, _, eye = _chunk_masks(reverse)
    gt = g_ref[...] + bias_ref[...]
    base = 2 * heads if reverse else 0
    for h in range(heads):
        i_col = jnp.where(valid, gt[:, base + h:base + h + 1], NEG)
        f_col = jnp.where(valid, _log_sigmoid(gt[:, base + heads + h:base + heads + h + 1]), 0.0)
        q = q_ref[:, h * dk:(h + 1) * dk]
        k = k_ref[:, h * dk:(h + 1) * dk] * (dk ** -0.5)
        v = v_ref[:, h * dv:(h + 1) * dv]
        ct = c_ref[h]
        n_row = n_ref[h]
        m = m_ref[h][:, 0:1]
        i_row = _to_row(i_col, eye)
        b_col, b_row = _scan_sums(f_col, tri, tri_t, eye)
        dmat = jnp.where(tri, b_col - b_row + i_row, NEG)
        inter = b_col + m
        m_t = jnp.maximum(inter, jnp.max(dmat, axis=1, keepdims=True))
        w = jnp.exp(dmat - m_t) * _dot_nt(q, k)
        s_inter = jnp.exp(inter - m_t)
        num = s_inter * _dot(q, ct) + _dot(w, v)
        den = s_inter * jnp.sum(q * n_row, axis=1, keepdims=True) + jnp.sum(w, axis=1, keepdims=True)
        hout = num / jnp.maximum(jnp.abs(den), jnp.exp(-m_t))
        b_last = jnp.sum(f_col, axis=0, keepdims=True)
        dec = b_last - b_col + i_col
        m_new = jnp.maximum(b_last + m, jnp.max(dec, axis=0, keepdims=True))
        a_old = jnp.exp(b_last + m - m_new)
        w_in = jnp.exp(dec - m_new)
        c_ref[h] = a_old * ct + _dot_tn(k, w_in * v)
        n_ref[h] = a_old * n_row + jnp.sum(w_in * k, axis=0, keepdims=True)
        m_ref[h] = jnp.broadcast_to(m_new, (1, LANE))
        cols = slice(h * dv, (h + 1) * dv)
        if reverse:
            y = yf_ref[:, cols] + hout
            o_ref[:, cols] = _head_norm_gate(y, nrm_ref[:, cols], _sigmoid(ao_ref[:, cols]), valid).astype(o_ref.dtype)
        else:
            o_ref[:, cols] = hout


def _mlstm(proj, gates, bias_row, nc, heads, dk, dv, yf=None, norm=None):
    reverse = yf is not None
    m = proj.shape[0]
    b = m // (nc * CHUNK)
    wq, wv = heads * dk, heads * dv

    def row(i, c):
        return i * nc + (nc - 1 - c if reverse else c)

    in_specs = [
        pl.BlockSpec((CHUNK, wq), lambda i, c: (row(i, c), 0)),
        pl.BlockSpec((CHUNK, wq), lambda i, c: (row(i, c), 1)),
        pl.BlockSpec((CHUNK, wv), lambda i, c: (row(i, c), (2 * wq) // wv)),
        pl.BlockSpec((CHUNK, LANE), lambda i, c: (row(i, c), 0)),
        pl.BlockSpec((1, LANE), lambda i, c: (0, 0)),
    ]
    args = [proj, proj, proj, gates, bias_row]
    if reverse:
        in_specs += [
            pl.BlockSpec((CHUNK, wv), lambda i, c: (row(i, c), 0)),
            pl.BlockSpec((CHUNK, wv), lambda i, c: (row(i, c), (2 * wq) // wv + 1)),
            pl.BlockSpec((1, wv), lambda i, c: (0, 0)),
        ]
        args += [yf, proj, norm.reshape(1, wv).astype(F32)]
    return pl.pallas_call(
        functools.partial(_mlstm_kernel, heads=heads, dk=dk, dv=dv, reverse=reverse),
        out_shape=jax.ShapeDtypeStruct((m, wv), BF16 if reverse else F32),
        grid=(b, nc),
        in_specs=in_specs,
        out_specs=pl.BlockSpec((CHUNK, wv), lambda i, c: (row(i, c), 0)),
        scratch_shapes=[
            pltpu.VMEM((heads, dk, dv), F32),
            pltpu.VMEM((heads, 1, dk), F32),
            pltpu.VMEM((heads, 1, LANE), F32),
        ],
        compiler_params=_cp(("parallel", "arbitrary")),
        name="mlstm_bwd" if reverse else "mlstm_fwd",
    )(*args)


def _unit_lower_inverse(a, eye):
    p = jnp.where(eye, 1.0, 0.0) - a
    x = a
    n = 1
    while 2 * n < CHUNK:
        x = _dot(x, x)
        p = p + _dot(p, x)
        n *= 2
    return p


def _gdn_kernel(*refs, unit, heads, dk, dv, gate_off, reverse):
    if reverse:
        q_ref, k_ref, v_ref, g_ref, dt_ref, alog_ref, of_ref, z_ref, nrm_ref, o_ref, s_ref = refs
    else:
        q_ref, k_ref, v_ref, g_ref, dt_ref, alog_ref, o_ref, s_ref = refs
    grp = pl.program_id(1)
    step = pl.program_id(2)
    chunk = pl.num_programs(2) - 1 - step if reverse else step

    @pl.when(step == 0)
    def _():
        s_ref[...] = jnp.zeros_like(s_ref)

    valid = _valid_rows(chunk)
    tri, tri_t, strict, eye = _chunk_masks(reverse)
    raw = g_ref[...]
    decay_all = -jnp.exp(alog_ref[...]) * _softplus(raw + dt_ref[...])
    beta_all = _sigmoid(raw)
    base = gate_off + (2 * heads if reverse else 0)
    for u in range(unit):
        h = grp * unit + u
        g_col = jnp.where(valid, _pick_col(decay_all, base + h), 0.0)
        beta = jnp.where(valid, _pick_col(beta_all, base + heads + h), 0.0)
        q = q_ref[:, u * dk:(u + 1) * dk]
        k = k_ref[:, u * dk:(u + 1) * dk]
        v = v_ref[:, u * dv:(u + 1) * dv]
        s = s_ref[u]
        b_col, b_row = _scan_sums(g_col, tri, tri_t, eye)
        decay = jnp.exp(jnp.where(tri, b_col - b_row, NEG))
        kb = k * beta
        a_mat = jnp.where(strict, _dot_nt(kb, k) * decay, 0.0)
        eb = jnp.exp(b_col)
        rhs = jnp.concatenate([v * beta, kb * eb], axis=1)
        sol = _dot(_unit_lower_inverse(a_mat, eye), rhs)
        v_new = sol[:, :dv] - _dot(sol[:, dv:], s)
        attn = _dot_nt(q, k) * decay
        o = _dot(q * eb, s) + _dot(attn, v_new)
        b_last = jnp.sum(g_col, axis=0, keepdims=True)
        s_ref[u] = s * jnp.exp(b_last) + _dot_tn(k * jnp.exp(b_last - b_col), v_new)
        cols = slice(u * dv, (u + 1) * dv)
        if reverse:
            y = of_ref[:, cols] + o
            o_ref[:, cols] = _head_norm_gate(y, nrm_ref[...], _silu(z_ref[:, cols]), valid).astype(o_ref.dtype)
        else:
            o_ref[:, cols] = o


def _gdn(qkv, gates, dt_row, alog_row, nc, heads, dk, dv, gate_off, of=None, z=None, z_off=0, norm=None):
    reverse = of is not None
    m = qkv.shape[0]
    b = m // (nc * CHUNK)
    unit = 4 if heads % 4 == 0 else (2 if heads % 2 == 0 else 1)
    ng = heads // unit

    def row(i, c):
        return i * nc + (nc - 1 - c if reverse else c)

    in_specs = [
        pl.BlockSpec((CHUNK, unit * dk), lambda i, g, c: (row(i, c), g)),
        pl.BlockSpec((CHUNK, unit * dk), lambda i, g, c: (row(i, c), ng + g)),
        pl.BlockSpec((CHUNK, unit * dv), lambda i, g, c: (row(i, c), (2 * heads * dk) // (unit * dv) + g)),
        pl.BlockSpec((CHUNK, LANE), lambda i, g, c: (row(i, c), 0)),
        pl.BlockSpec((1, LANE), lambda i, g, c: (0, 0)),
        pl.BlockSpec((1, LANE), lambda i, g, c: (0, 0)),
    ]
    args = [qkv, qkv, qkv, gates, dt_row, alog_row]
    if reverse:
        zb = z_off // (unit * dv)
        in_specs += [
            pl.BlockSpec((CHUNK, unit * dv), lambda i, g, c: (row(i, c), g)),
            pl.BlockSpec((CHUNK, unit * dv), lambda i, g, c: (row(i, c), zb + g)),
            pl.BlockSpec((1, dv), lambda i, g, c: (0, 0)),
        ]
        args += [of, z, norm.reshape(1, dv).astype(F32)]
    return pl.pallas_call(
        functools.partial(_gdn_kernel, unit=unit, heads=heads, dk=dk, dv=dv, gate_off=gate_off, reverse=reverse),
        out_shape=jax.ShapeDtypeStruct((m, heads * dv), BF16 if reverse else F32),
        grid=(b, ng, nc),
        in_specs=in_specs,
        out_specs=pl.BlockSpec((CHUNK, unit * dv), lambda i, g, c: (row(i, c), g)),
        scratch_shapes=[pltpu.VMEM((unit, dk, dv), F32)],
        compiler_params=_cp(("parallel", "parallel", "arbitrary")),
        name="gdn_bwd" if reverse else "gdn_fwd",
    )(*args)


def _seg_cumsum(x, reverse):
    rows = x.shape[0]
    r = _iota((rows, 1), 0) % SUB
    s = 1
    while s < SUB:
        if reverse:
            x = x + jnp.where(r < SUB - s, pltpu.roll(x, rows - s, 0), 0.0)
        else:
            x = x + jnp.where(r >= s, pltpu.roll(x, s, 0), 0.0)
        s *= 2
    return x


def _gla_kernel(*refs, unit, dk, dv, layer, reverse):
    if reverse:
        q_ref, f_ref, v_ref, lb_ref, of_ref, z_ref, nrm_ref, o_ref, s_ref = refs
    else:
        q_ref, f_ref, v_ref, lb_ref, o_ref, s_ref = refs
    step = pl.program_id(2)
    chunk = pl.num_programs(2) - 1 - step if reverse else step

    @pl.when(step == 0)
    def _():
        s_ref[...] = jnp.zeros_like(s_ref)

    valid = _valid_rows(chunk)
    lbm = lb_ref[...]
    e = jnp.exp(lbm - jnp.max(lbm, axis=0, keepdims=True))
    lb = jnp.sum(e[1:layer + 1], axis=0, keepdims=True) / jnp.sum(e, axis=0, keepdims=True)
    la = jnp.log(lb)
    lc = jnp.log1p(-lb) + _log_sigmoid(f_ref[...])
    logf = jnp.maximum(la, lc) + jnp.log1p(jnp.exp(-jnp.abs(la - lc)))
    key = jnp.where(valid, 1.0 - jnp.exp(logf), 0.0)
    logf = jnp.where(valid, logf, 0.0)
    q = _silu(q_ref[...])
    cs = _seg_cumsum(logf, reverse)
    qe = q * jnp.exp(cs)
    nsub = CHUNK // SUB
    r16 = _iota((SUB, 1), 0)
    lane = _iota((SUB, LANE), 1)
    order = range(nsub - 1, -1, -1) if reverse else range(nsub)
    for u in range(unit):
        st = s_ref[u]
        kc = slice(u * dk, (u + 1) * dk)
        vc = slice(u * dv, (u + 1) * dv)
        for i in order:
            rows = slice(i * SUB, (i + 1) * SUB)
            cs_i = cs[rows, kc]
            last = 0 if reverse else SUB - 1
            tot = cs_i[last:last + 1]
            key_i = key[rows, kc]
            q_i = q[rows, kc]
            v_i = v_ref[rows, vc]
            attn_t = jnp.zeros((SUB, LANE), F32)
            for t in range(SUB):
                ok = (r16 >= t) if reverse else (r16 <= t)
                p = q_i[t:t + 1] * key_i * jnp.exp(jnp.where(ok, cs_i[t:t + 1] - cs_i, NEG))
                attn_t = jnp.where(lane == t, jnp.sum(p, axis=1, keepdims=True), attn_t)
            o = _dot_nt(qe[rows, kc], st) + _dot_tn(attn_t, v_i)[:SUB]
            st = st * jnp.exp(tot) + _dot_tn(v_i, key_i * jnp.exp(tot - cs_i))
            if reverse:
                y = of_ref[rows, vc] + o
                valid_i = (chunk * CHUNK + i * SUB + r16) >= PAD
                o_ref[rows, vc] = _head_norm_gate(y, nrm_ref[...], _silu(z_ref[rows, vc]), valid_i).astype(o_ref.dtype)
            else:
                o_ref[rows, vc] = o
        s_ref[u] = st


def _gla(proj, lower_bounds, nc, heads, dk, dv, layer, of=None, norm=None):
    reverse = of is not None
    m = proj.shape[0]
    b = m // (nc * CHUNK)
    unit = 4 if heads % 4 == 0 else (2 if heads % 2 == 0 else 1)
    ng = heads // unit
    depth = lower_bounds.shape[0]
    assert dk == dv

    def row(i, c):
        return i * nc + (nc - 1 - c if reverse else c)

    in_specs = [
        pl.BlockSpec((CHUNK, unit * dk), lambda i, g, c: (row(i, c), g)),
        pl.BlockSpec((CHUNK, unit * dk), lambda i, g, c: (row(i, c), (2 if reverse else 1) * ng + g)),
        pl.BlockSpec((CHUNK, unit * dv), lambda i, g, c: (row(i, c), 3 * ng + g)),
        pl.BlockSpec((depth, unit * dk), lambda i, g, c: (0, g)),
    ]
    args = [proj, proj, proj, lower_bounds.astype(F32)]
    if reverse:
        in_specs += [
            pl.BlockSpec((CHUNK, unit * dv), lambda i, g, c: (row(i, c), g)),
            pl.BlockSpec((CHUNK, unit * dv), lambda i, g, c: (row(i, c), 4 * ng + g)),
            pl.BlockSpec((1, dv), lambda i, g, c: (0, 0)),
        ]
        args += [of, proj, norm.reshape(1, dv).astype(F32)]
    return pl.pallas_call(
        functools.partial(_gla_kernel, unit=unit, dk=dk, dv=dv, layer=layer, reverse=reverse),
        out_shape=jax.ShapeDtypeStruct((m, heads * dv), BF16 if reverse else F32),
        grid=(b, ng, nc),
        in_specs=in_specs,
        out_specs=pl.BlockSpec((CHUNK, unit * dv), lambda i, g, c: (row(i, c), g)),
        scratch_shapes=[pltpu.VMEM((unit, dv, dk), F32)],
        compiler_params=_cp(("parallel", "parallel", "arbitrary")),
        name="hgrn2_bwd" if reverse else "hgrn2_fwd",
    )(*args)


def _diff_attn_kernel(q_ref, k_ref, v_ref, qn_ref, kn_ref, lam_ref, sn_ref, o_ref, kn_s, v_s, *, hd, heads, lam_init):
    h = pl.program_id(1)
    qi = pl.program_id(2)
    tq = q_ref.shape[0]
    t = k_ref.shape[0]

    @pl.when(qi == 0)
    def _():
        kk = k_ref[...]
        for mp in range(2):
            km = kk[:, mp * hd:(mp + 1) * hd]
            ms = jnp.mean(km * km, axis=1, keepdims=True)
            kn_s[mp] = (km * lax.rsqrt(ms + EPS) * kn_ref[...]).astype(BF16)
        v_s[...] = v_ref[...].astype(BF16)

    lv = lam_ref[...]
    lam = (jnp.exp(jnp.sum(lv[0:1] * lv[1:2], axis=1, keepdims=True))
           - jnp.exp(jnp.sum(lv[2:3] * lv[3:4], axis=1, keepdims=True)) + lam_init)
    slope = jnp.exp2(-8.0 * (jnp.full((1, 1), h + 1, jnp.int32).astype(F32)) / heads)
    qpos = qi * tq + _iota((tq, t), 0)
    kpos = _iota((tq, t), 1)
    bias = slope * jnp.abs(qpos - kpos).astype(F32)
    kvalid = kpos >= PAD
    q = q_ref[...]
    probs = []
    for mp in range(2):
        qm = q[:, mp * hd:(mp + 1) * hd]
        ms = jnp.mean(qm * qm, axis=1, keepdims=True)
        qn = qm * lax.rsqrt(ms + EPS) * qn_ref[...]
        s = lax.dot_general(qn.astype(BF16), kn_s[mp], (((1,), (1,)), ((), ())), preferred_element_type=F32)
        s = jnp.where(kvalid, s * (hd ** -0.5) - bias, NEG)
        e = jnp.exp(s - jnp.max(s, axis=1, keepdims=True))
        probs.append(e / jnp.sum(e, axis=1, keepdims=True))
    a = probs[0] - lam * probs[1]
    o = jnp.dot(a.astype(BF16), v_s[...], preferred_element_type=F32)
    ms = jnp.mean(o * o, axis=1, keepdims=True)
    o = o * lax.rsqrt(ms + EPS) * sn_ref[...] * (1.0 - lam_init)
    qvalid = (qi * tq + _iota((tq, 1), 0)) >= PAD
    o_ref[...] = jnp.where(qvalid, o, 0.0).astype(o_ref.dtype)


def _diff_attn(proj, col_off, t, heads, hd, q_norm, k_norm, lam_vec, sub_norm, lam_init):
    m = proj.shape[0]
    b = m // t
    w = 2 * hd
    tq = _tile(t, (264, 192, 64))
    nq = t // tq
    c0 = col_off // w
    return pl.pallas_call(
        functools.partial(_diff_attn_kernel, hd=hd, heads=heads, lam_init=lam_init),
        out_shape=jax.ShapeDtypeStruct((m, heads * w), BF16),
        grid=(b, heads, nq),
        in_specs=[
            pl.BlockSpec((tq, w), lambda i, h, j: (i * nq + j, c0 + h)),
            pl.BlockSpec((t, w), lambda i, h, j: (i, c0 + heads + h)),
            pl.BlockSpec((t, w), lambda i, h, j: (i, c0 + 2 * heads + h)),
            pl.BlockSpec((1, hd), lambda i, h, j: (0, 0)),
            pl.BlockSpec((1, hd), lambda i, h, j: (0, 0)),
            pl.BlockSpec((4, hd), lambda i, h, j: (0, 0)),
            pl.BlockSpec((1, w), lambda i, h, j: (0, 0)),
        ],
        out_specs=pl.BlockSpec((tq, w), lambda i, h, j: (i * nq + j, h)),
        scratch_shapes=[pltpu.VMEM((2, t, hd), BF16), pltpu.VMEM((t, w), BF16)],
        compiler_params=_cp(("parallel", "parallel", "arbitrary")),
        name="diff_attn",
    )(proj, proj, proj, q_norm.reshape(1, hd).astype(F32), k_norm.reshape(1, hd).astype(F32),
      lam_vec.astype(F32), sub_norm.reshape(1, w).astype(F32))


def _place(width, pieces):
    row = jnp.zeros((width,), F32)
    for off, vals in pieces:
        row = lax.dynamic_update_slice(row, vals.astype(F32).reshape(-1), (off,))
    return row.reshape(1, width)


def _trunk(h, t, p):
    m, d = h.shape
    nc = t // CHUNK
    mix_w = d // 2
    depth = p["norm_mix"].shape[0]
    ha = p["mlstm_i_bias"].shape[-1]
    a_dv = mix_w // ha
    a_dqk = a_dv // 2
    hb = p["dn_a_log"].shape[-1]
    b_dk = mix_w // hb
    c_dv = p["hgrn_norm"].shape[-1]
    hc = mix_w // c_dv
    c_dk = p["hgrn_lower_bounds"].shape[-1] // hc
    d_hd = p["diff_q_norm"].shape[-1]
    hdh = mix_w // (2 * d_hd)
    o_ag = 2 * ha * a_dqk + 2 * ha * a_dv
    o_qkv = o_ag + 4 * ha
    o_bz = o_qkv + 3 * mix_w
    o_bg = o_bz + mix_w
    assert 4 * ha + 4 * hb <= LANE
    for l in range(depth):
        hn = _rmsnorm_bf16(h, p["norm_mix"][l])
        if l % 2 == 0:
            e = l // 2
            w_in = p["ab_w_in"][e]
            w_plain = jnp.concatenate([w_in[:, :o_ag], w_in[:, o_bz:o_bg]], axis=1).astype(BF16)
            w_gate = jnp.concatenate(
                [w_in[:, o_ag:o_qkv], w_in[:, o_bg:], jnp.zeros((d, LANE - 4 * ha - 4 * hb), F32)], axis=1).astype(BF16)
            w_qkv = w_in[:, o_qkv:o_bz].astype(BF16)
            proj = _matmul(hn, w_plain, name="ab_proj")
            gates = _matmul(hn, w_gate, name="ab_gates")
            qkv = _qkv_conv(hn, w_qkv, p["dn_conv"][e], t, b_dk)
            bias_row = _place(LANE, [(dd * 2 * ha + gg * ha, (p["mlstm_i_bias"][e], p["mlstm_f_bias"][e])[gg][dd])
                                     for dd in range(2) for gg in range(2)])
            yf = _mlstm(proj, gates, bias_row, nc, ha, a_dqk, a_dv)
            xa = _mlstm(proj, gates, bias_row, nc, ha, a_dqk, a_dv, yf=yf, norm=p["mlstm_norm"][e])
            dt_row = _place(LANE, [(4 * ha + dd * 2 * hb, p["dn_dt_bias"][e][dd]) for dd in range(2)])
            alog_row = _place(LANE, [(4 * ha + dd * 2 * hb, p["dn_a_log"][e][dd]) for dd in range(2)])
            of = _gdn(qkv, gates, dt_row, alog_row, nc, hb, b_dk, b_dk, 4 * ha)
            xb = _gdn(qkv, gates, dt_row, alog_row, nc, hb, b_dk, b_dk, 4 * ha, of=of, z=proj, z_off=o_ag,
                      norm=p["dn_norm"][e])
            w_out = p["ab_w_out"][e].astype(BF16)
        else:
            o = l // 2
            proj = _matmul(hn, p["cd_w_in"][o].astype(BF16), name="cd_proj")
            of = _gla(proj, p["hgrn_lower_bounds"], nc, hc, c_dk, c_dv, l)
            xa = _gla(proj, p["hgrn_lower_bounds"], nc, hc, c_dk, c_dv, l, of=of, norm=p["hgrn_norm"][o])
            lam_init = 0.8 - 0.6 * math.exp(-0.3 * l)
            xb = _diff_attn(proj, 3 * hc * c_dk + 2 * hc * c_dv, t, hdh, d_hd, p["diff_q_norm"][o],
                            p["diff_k_norm"][o], p["diff_lambda"][o], p["diff_norm"][o], lam_init)
            w_out = p["cd_w_out"][o].astype(BF16)
        h = _matmul2_residual(xa, xb, w_out, h)
        hn = _rmsnorm_bf16(h, p["norm_ffn"][l])
        act = _ffn_up(hn, p["ffn_w_up"][l].astype(BF16), p["ffn_conv_w"][l], p["ffn_conv_b"][l], t)
        h = _matmul_residual(act, p["ffn_w_down"][l].astype(BF16), h)
    return h


def kernel(x_prompt, x_sample, meta_tokens, norm_mix, norm_ffn, ab_w_in, ab_w_out, mlstm_i_bias, mlstm_f_bias, mlstm_norm, dn_conv, dn_a_log, dn_dt_bias, dn_norm, cd_w_in, cd_w_out, hgrn_lower_bounds, hgrn_norm, diff_q_norm, diff_k_norm, diff_lambda, diff_norm, ffn_w_up, ffn_conv_w, ffn_conv_b, ffn_w_down):
    p = dict(norm_mix=norm_mix, norm_ffn=norm_ffn, ab_w_in=ab_w_in, ab_w_out=ab_w_out, mlstm_i_bias=mlstm_i_bias,
             mlstm_f_bias=mlstm_f_bias, mlstm_norm=mlstm_norm, dn_conv=dn_conv, dn_a_log=dn_a_log,
             dn_dt_bias=dn_dt_bias, dn_norm=dn_norm, cd_w_in=cd_w_in, cd_w_out=cd_w_out,
             hgrn_lower_bounds=hgrn_lower_bounds, hgrn_norm=hgrn_norm, diff_q_norm=diff_q_norm,
             diff_k_norm=diff_k_norm, diff_lambda=diff_lambda, diff_norm=diff_norm, ffn_w_up=ffn_w_up,
             ffn_conv_w=ffn_conv_w, ffn_conv_b=ffn_conv_b, ffn_w_down=ffn_w_down)
    bp, s, d = x_prompt.shape
    bs = x_sample.shape[0]
    assert x_sample.shape[1:] == (s, d)
    t = PAD + N_META + s
    assert t % CHUNK == 0
    x = jnp.concatenate([x_prompt, x_sample], axis=0).astype(F32)
    b = bp + bs
    front = jnp.concatenate([jnp.zeros((PAD, d), F32), meta_tokens.astype(F32)], axis=0)
    h = jnp.concatenate([jnp.broadcast_to(front[None], (b, PAD + N_META, d)), x], axis=1).reshape(b * t, d)
    h = _trunk(h, t, p)
    y = h.reshape(b, t, d)[:, PAD + N_META:]
    return (y[:bp], y[bp:])
```

```python
import functools
import math

import jax
import jax.numpy as jnp
from jax import lax
from jax.experimental import pallas as pl
from jax.experimental.pallas import tpu as pltpu

F32 = jnp.float32
BF16 = jnp.bfloat16
EPS = 1e-6
NEG = -1e30
N_META = 16
CHUNK = 64
PAD = CHUNK - N_META
SUB = 16
GLA_SAFE_DECAY = 60.0
LANE = 128
LOG2E = 1.4426950408889634
VMEM_LIMIT = 56 * 1024 * 1024


def _cp(sem, vmem=VMEM_LIMIT):
    return pltpu.CompilerParams(dimension_semantics=sem, vmem_limit_bytes=vmem)


def _iota(shape, dim):
    return lax.broadcasted_iota(jnp.int32, shape, dim)


def _dot(a, b):
    return jnp.dot(a.astype(BF16), b.astype(BF16), preferred_element_type=F32)


def _dot_nt(a, b):
    return lax.dot_general(a.astype(BF16), b.astype(BF16), (((1,), (1,)), ((), ())), preferred_element_type=F32)


def _dot_tn(a, b):
    return lax.dot_general(a.astype(BF16), b.astype(BF16), (((0,), (0,)), ((), ())), preferred_element_type=F32)


def _log_sigmoid(x):
    return jnp.minimum(x, 0.0) - jnp.log1p(jnp.exp(-jnp.abs(x)))


def _softplus(x):
    return jnp.maximum(x, 0.0) + jnp.log1p(jnp.exp(-jnp.abs(x)))


def _sigmoid(x):
    return 1.0 / (1.0 + jnp.exp(-x))


def _silu(x):
    return x * _sigmoid(x)


def _tile(n, cands):
    for c in cands:
        if n % c == 0:
            return c
    return n


def _rms_kernel(x_ref, g_ref, o_ref):
    x = x_ref[...]
    ms = jnp.mean(x * x, axis=-1, keepdims=True)
    o_ref[...] = (x * lax.rsqrt(ms + EPS) * g_ref[...]).astype(o_ref.dtype)


def _rmsnorm_bf16(x, g):
    m, d = x.shape
    tm = _tile(m, (528, 264, 192, 64))
    return pl.pallas_call(
        _rms_kernel,
        out_shape=jax.ShapeDtypeStruct((m, d), BF16),
        grid=(m // tm,),
        in_specs=[pl.BlockSpec((tm, d), lambda i: (i, 0)), pl.BlockSpec((1, d), lambda i: (0, 0))],
        out_specs=pl.BlockSpec((tm, d), lambda i: (i, 0)),
        compiler_params=_cp(("parallel",)),
        name="rmsnorm",
    )(x, g.reshape(1, d).astype(F32))


def _mm_kernel(x_ref, w_ref, o_ref):
    o_ref[...] = jnp.dot(x_ref[...], w_ref[...], preferred_element_type=F32).astype(o_ref.dtype)


def _matmul(x, w, out_dtype=F32, name="matmul"):
    m, k = x.shape
    n = w.shape[1]
    tm = _tile(m, (1056, 704, 192, 64))
    tn = _tile(n, (512, 256, 128))
    return pl.pallas_call(
        _mm_kernel,
        out_shape=jax.ShapeDtypeStruct((m, n), out_dtype),
        grid=(m // tm, n // tn),
        in_specs=[pl.BlockSpec((tm, k), lambda i, j: (i, 0)), pl.BlockSpec((k, tn), lambda i, j: (0, j))],
        out_specs=pl.BlockSpec((tm, tn), lambda i, j: (i, j)),
        compiler_params=_cp(("parallel", "arbitrary")),
        name=name,
    )(x, w)


def _mm2_res_kernel(xa_ref, xb_ref, wa_ref, wb_ref, r_ref, o_ref):
    acc = jnp.dot(xa_ref[...], wa_ref[...], preferred_element_type=F32)
    acc = acc + jnp.dot(xb_ref[...], wb_ref[...], preferred_element_type=F32)
    o_ref[...] = r_ref[...] + acc


def _matmul2_residual(xa, xb, w, res, name="mix_out"):
    m, ka = xa.shape
    kb = xb.shape[1]
    n = w.shape[1]
    tm = _tile(m, (1056, 704, 192, 64))
    tn = _tile(n, (512, 256, 128))
    assert ka == kb
    return pl.pallas_call(
        _mm2_res_kernel,
        out_shape=jax.ShapeDtypeStruct((m, n), F32),
        grid=(m // tm, n // tn),
        in_specs=[
            pl.BlockSpec((tm, ka), lambda i, j: (i, 0)),
            pl.BlockSpec((tm, kb), lambda i, j: (i, 0)),
            pl.BlockSpec((ka, tn), lambda i, j: (0, j)),
            pl.BlockSpec((kb, tn), lambda i, j: (1, j)),
            pl.BlockSpec((tm, tn), lambda i, j: (i, j)),
        ],
        out_specs=pl.BlockSpec((tm, tn), lambda i, j: (i, j)),
        compiler_params=_cp(("parallel", "arbitrary")),
        name=name,
    )(xa, xb, w, w, res)


def _mm_res_kernel(x_ref, w_ref, r_ref, o_ref):
    o_ref[...] = r_ref[...] + jnp.dot(x_ref[...], w_ref[...], preferred_element_type=F32)


def _matmul_residual(x, w, res, name="ffn_down"):
    m, k = x.shape
    n = w.shape[1]
    tm = _tile(m, (704, 192, 64))
    tn = _tile(n, (256, 128))
    return pl.pallas_call(
        _mm_res_kernel,
        out_shape=jax.ShapeDtypeStruct((m, n), F32),
        grid=(m // tm, n // tn),
        in_specs=[
            pl.BlockSpec((tm, k), lambda i, j: (i, 0)),
            pl.BlockSpec((k, tn), lambda i, j: (0, j)),
            pl.BlockSpec((tm, tn), lambda i, j: (i, j)),
        ],
        out_specs=pl.BlockSpec((tm, tn), lambda i, j: (i, j)),
        compiler_params=_cp(("parallel", "arbitrary")),
        name=name,
    )(x, w, res)


def _conv3_rows(y, cw):
    t = y.shape[0]
    return cw[0:1] * pltpu.roll(y, 1, 0) + cw[1:2] * y + cw[2:3] * pltpu.roll(y, t - 1, 0)


def _ffn_up_kernel(x_ref, wg_ref, wv_ref, cw_ref, cb_ref, o_ref):
    x = x_ref[...]
    g = jnp.dot(x, wg_ref[...], preferred_element_type=F32)
    v = jnp.dot(x, wv_ref[...], preferred_element_type=F32)
    gc = _conv3_rows(g, cw_ref[...]) + cb_ref[...]
    o_ref[...] = (_silu(gc) * v).astype(o_ref.dtype)


def _ffn_up(xn, w_up, conv_w, conv_b, t):
    m, d = xn.shape
    f = conv_w.shape[1]
    tn = _tile(f, (256, 128))
    nj = f // tn
    return pl.pallas_call(
        _ffn_up_kernel,
        out_shape=jax.ShapeDtypeStruct((m, f), BF16),
        grid=(m // t, nj),
        in_specs=[
            pl.BlockSpec((t, d), lambda i, j: (i, 0), pipeline_mode=pl.Buffered(1)),
            pl.BlockSpec((d, tn), lambda i, j: (0, j)),
            pl.BlockSpec((d, tn), lambda i, j: (0, j + nj)),
            pl.BlockSpec((3, tn), lambda i, j: (0, j)),
            pl.BlockSpec((1, tn), lambda i, j: (0, j)),
        ],
        out_specs=pl.BlockSpec((t, tn), lambda i, j: (i, j)),
        compiler_params=_cp(("parallel", "arbitrary")),
        name="ffn_up",
    )(xn, w_up, w_up, conv_w.astype(F32), conv_b.reshape(1, f).astype(F32))


def _qkv_conv_kernel(x_ref, w_ref, cw_ref, o_ref, *, nq, qscale):
    j = pl.program_id(1)
    y = jnp.dot(x_ref[...], w_ref[...], preferred_element_type=F32)
    a = _silu(_conv3_rows(y, cw_ref[...]))
    t, tn = a.shape
    a = jnp.where(_iota((t, 1), 0) >= PAD, a, 0.0)
    is_qk = j < 2 * nq
    scale = jnp.where(j < nq, qscale, 1.0)
    for s in range(tn // LANE):
        seg = a[:, s * LANE:(s + 1) * LANE]
        ss = jnp.sum(seg * seg, axis=1, keepdims=True)
        nrm = seg * lax.rsqrt(ss + EPS) * scale
        o_ref[:, s * LANE:(s + 1) * LANE] = jnp.where(is_qk, nrm, seg)


def _qkv_conv(xn, w, conv_w, t, dk):
    m, d = xn.shape
    n = w.shape[1]
    tn = _tile(n // 3, (256, 128))
    return pl.pallas_call(
        functools.partial(_qkv_conv_kernel, nq=(n // 3) // tn, qscale=dk ** -0.5),
        out_shape=jax.ShapeDtypeStruct((m, n), F32),
        grid=(m // t, n // tn),
        in_specs=[
            pl.BlockSpec((t, d), lambda i, j: (i, 0), pipeline_mode=pl.Buffered(1)),
            pl.BlockSpec((d, tn), lambda i, j: (0, j)),
            pl.BlockSpec((3, tn), lambda i, j: (0, j)),
        ],
        out_specs=pl.BlockSpec((t, tn), lambda i, j: (i, j)),
        compiler_params=_cp(("parallel", "arbitrary")),
        name="dn_qkv_conv",
    )(xn, w, conv_w.astype(F32))


def _chunk_masks(reverse):
    r = _iota((CHUNK, CHUNK), 0)
    s = _iota((CHUNK, CHUNK), 1)
    tri = (s >= r) if reverse else (s <= r)
    strict = (s > r) if reverse else (s < r)
    return tri, strict, r == s


def _scan_rows(x, seg, reverse):
    rows = x.shape[0]
    r = _iota((rows, 1), 0) % seg
    s = 1
    while s < seg:
        if reverse:
            x = x + jnp.where(r < seg - s, pltpu.roll(x, rows - s, 0), 0.0)
        else:
            x = x + jnp.where(r >= s, pltpu.roll(x, s, 0), 0.0)
        s *= 2
    return x


def _transpose_tile(x):
    return jnp.concatenate([x, jnp.zeros((LANE - x.shape[0], LANE), F32)], axis=0).T


def _valid_rows(chunk):
    return (chunk * CHUNK + _iota((CHUNK, 1), 0)) >= PAD


def _head_norm_gate(y, gain, gate, valid):
    ms = jnp.mean(y * y, axis=1, keepdims=True)
    return jnp.where(valid, y * lax.rsqrt(ms + EPS) * gain * gate, 0.0)


def _mlstm_kernel(*refs, heads, dk, dv, reverse):
    if reverse:
        q_ref, k_ref, v_ref, g_ref, bias_ref, yf_ref, ao_ref, nrm_ref, o_ref, c_ref, n_ref, m_ref = refs
    else:
        q_ref, k_ref, v_ref, g_ref, bias_ref, o_ref, c_ref, n_ref, m_ref = refs
    step = pl.program_id(1)
    chunk = pl.num_programs(1) - 1 - step if reverse else step

    @pl.when(step == 0)
    def _():
        c_ref[...] = jnp.zeros_like(c_ref)
        n_ref[...] = jnp.zeros_like(n_ref)
        m_ref[...] = jnp.full_like(m_ref, NEG)

    valid = _valid_rows(chunk)
    tri, _, _ = _chunk_masks(reverse)
    gt = g_ref[...] + bias_ref[...]
    i_all = jnp.where(valid, gt, NEG)
    f_all = jnp.where(valid, _log_sigmoid(gt), 0.0)
    cum = _scan_rows(f_all, CHUNK, reverse)
    i_t = _transpose_tile(i_all)
    cum_t = _transpose_tile(cum)
    last = 0 if reverse else CHUNK - 1
    base = 2 * heads if reverse else 0
    hs = range(heads)
    ci = [base + h for h in hs]
    cf = [base + heads + h for h in hs]
    i_col = [i_all[:, ci[h]:ci[h] + 1] for h in hs]
    i_row = [i_t[ci[h]:ci[h] + 1, :CHUNK] for h in hs]
    b_col = [cum[:, cf[h]:cf[h] + 1] for h in hs]
    b_row = [cum_t[cf[h]:cf[h] + 1, :CHUNK] for h in hs]
    b_last = [cum[last:last + 1, cf[h]:cf[h] + 1] for h in hs]
    q = [q_ref[:, h * dk:(h + 1) * dk] for h in hs]
    k = [k_ref[:, h * dk:(h + 1) * dk] * (dk ** -0.5) for h in hs]
    v = [v_ref[:, h * dv:(h + 1) * dv] for h in hs]
    ct = [c_ref[h] for h in hs]
    n_row = [n_ref[h] for h in hs]
    m = [m_ref[h][:, 0:1] for h in hs]
    qk = [_dot_nt(q[h], k[h]) for h in hs]
    qc = [_dot(q[h], ct[h]) for h in hs]
    dmat = [jnp.where(tri, b_col[h] - b_row[h] + i_row[h], NEG) for h in hs]
    inter = [b_col[h] + m[h] for h in hs]
    m_t = [jnp.maximum(inter[h], jnp.max(dmat[h], axis=1, keepdims=True)) for h in hs]
    w = [jnp.exp(dmat[h] - m_t[h]) * qk[h] for h in hs]
    s_inter = [jnp.exp(inter[h] - m_t[h]) for h in hs]
    wv = [_dot(w[h], v[h]) for h in hs]
    dec = [b_last[h] - b_col[h] + i_col[h] for h in hs]
    m_new = [jnp.maximum(b_last[h] + m[h], jnp.max(dec[h], axis=0, keepdims=True)) for h in hs]
    a_old = [jnp.exp(b_last[h] + m[h] - m_new[h]) for h in hs]
    w_in = [jnp.exp(dec[h] - m_new[h]) for h in hs]
    kv = [_dot_tn(k[h], w_in[h] * v[h]) for h in hs]
    for h in hs:
        num = s_inter[h] * qc[h] + wv[h]
        den = (s_inter[h] * jnp.sum(q[h] * n_row[h], axis=1, keepdims=True)
               + jnp.sum(w[h], axis=1, keepdims=True))
        hout = num / jnp.maximum(jnp.abs(den), jnp.exp(-m_t[h]))
        c_ref[h] = a_old[h] * ct[h] + kv[h]
        n_ref[h] = a_old[h] * n_row[h] + jnp.sum(w_in[h] * k[h], axis=0, keepdims=True)
        m_ref[h] = jnp.broadcast_to(m_new[h], (1, LANE))
        cols = slice(h * dv, (h + 1) * dv)
        if reverse:
            y = yf_ref[:, cols] + hout
            o_ref[:, cols] = _head_norm_gate(y, nrm_ref[:, cols], _sigmoid(ao_ref[:, cols]), valid).astype(o_ref.dtype)
        else:
            o_ref[:, cols] = hout


def _mlstm(proj, gates, bias_row, nc, heads, dk, dv, yf=None, norm=None):
    reverse = yf is not None
    m = proj.shape[0]
    b = m // (nc * CHUNK)
    wq, wv = heads * dk, heads * dv

    def row(i, c):
        return i * nc + (nc - 1 - c if reverse else c)

    in_specs = [
        pl.BlockSpec((CHUNK, wq), lambda i, c: (row(i, c), 0)),
        pl.BlockSpec((CHUNK, wq), lambda i, c: (row(i, c), 1)),
        pl.BlockSpec((CHUNK, wv), lambda i, c: (row(i, c), (2 * wq) // wv)),
        pl.BlockSpec((CHUNK, LANE), lambda i, c: (row(i, c), 0)),
        pl.BlockSpec((1, LANE), lambda i, c: (0, 0)),
    ]
    args = [proj, proj, proj, gates, bias_row]
    if reverse:
        in_specs += [
            pl.BlockSpec((CHUNK, wv), lambda i, c: (row(i, c), 0)),
            pl.BlockSpec((CHUNK, wv), lambda i, c: (row(i, c), (2 * wq) // wv + 1)),
            pl.BlockSpec((1, wv), lambda i, c: (0, 0)),
        ]
        args += [yf, proj, norm.reshape(1, wv).astype(F32)]
    return pl.pallas_call(
        functools.partial(_mlstm_kernel, heads=heads, dk=dk, dv=dv, reverse=reverse),
        out_shape=jax.ShapeDtypeStruct((m, wv), BF16 if reverse else F32),
        grid=(b, nc),
        in_specs=in_specs,
        out_specs=pl.BlockSpec((CHUNK, wv), lambda i, c: (row(i, c), 0)),
        scratch_shapes=[
            pltpu.VMEM((heads, dk, dv), F32),
            pltpu.VMEM((heads, 1, dk), F32),
            pltpu.VMEM((heads, 1, LANE), F32),
        ],
        compiler_params=_cp(("parallel", "arbitrary")),
        name="mlstm_bwd" if reverse else "mlstm_fwd",
    )(*args)


def _gdn_kernel(*refs, unit, heads, dk, dv, gate_off, reverse):
    if reverse:
        q_ref, k_ref, v_ref, g_ref, dt_ref, alog_ref, of_ref, z_ref, nrm_ref, o_ref, s_ref = refs
    else:
        q_ref, k_ref, v_ref, g_ref, dt_ref, alog_ref, o_ref, s_ref = refs
    grp = pl.program_id(1)
    step = pl.program_id(2)
    chunk = pl.num_programs(2) - 1 - step if reverse else step

    @pl.when(step == 0)
    def _():
        s_ref[...] = jnp.zeros_like(s_ref)

    valid = _valid_rows(chunk)
    tri, strict, eye = _chunk_masks(reverse)
    eye_f = jnp.where(eye, 1.0, 0.0)
    raw = g_ref[...]
    decay_all = jnp.where(valid, -jnp.exp(alog_ref[...]) * _softplus(raw + dt_ref[...]), 0.0)
    beta_all = jnp.where(valid, _sigmoid(raw), 0.0)
    cum = _scan_rows(decay_all, CHUNK, reverse)
    if unit != heads:
        shift = LANE - gate_off - (2 * heads if reverse else 0) - grp * unit
        cum = pltpu.roll(cum, shift, 1)
        beta_all = pltpu.roll(beta_all, shift, 1)
        base = 0
    else:
        base = gate_off + (2 * heads if reverse else 0)
    cum_t = _transpose_tile(cum)
    last = 0 if reverse else CHUNK - 1
    us = range(unit)
    b_col = [cum[:, base + u:base + u + 1] for u in us]
    b_row = [cum_t[base + u:base + u + 1, :CHUNK] for u in us]
    b_last = [cum[last:last + 1, base + u:base + u + 1] for u in us]
    beta = [beta_all[:, base + heads + u:base + heads + u + 1] for u in us]
    q = [q_ref[:, u * dk:(u + 1) * dk] for u in us]
    k = [k_ref[:, u * dk:(u + 1) * dk] for u in us]
    v = [v_ref[:, u * dv:(u + 1) * dv] for u in us]
    s = [s_ref[u] for u in us]
    kb = [k[u] * beta[u] for u in us]
    kk = [_dot_nt(kb[u], k[u]) for u in us]
    qk = [_dot_nt(q[u], k[u]) for u in us]
    decay = [jnp.exp(jnp.where(tri, b_col[u] - b_row[u], NEG)) for u in us]
    eb = [jnp.exp(b_col[u]) for u in us]
    qs = [_dot(q[u] * eb[u], s[u]) for u in us]
    x = [jnp.where(strict, kk[u] * decay[u], 0.0) for u in us]
    rhs = [jnp.concatenate([v[u] * beta[u], kb[u] * eb[u]], axis=1) for u in us]
    p = [eye_f - x[u] for u in us]
    n = 1
    while 2 * n < CHUNK:
        x = [_dot(x[u], x[u]) for u in us]
        p = [p[u] + _dot(p[u], x[u]) for u in us]
        n *= 2
    sol = [_dot(p[u], rhs[u]) for u in us]
    ws = [_dot(sol[u][:, dv:], s[u]) for u in us]
    v_new = [sol[u][:, :dv] - ws[u] for u in us]
    attn = [qk[u] * decay[u] for u in us]
    av = [_dot(attn[u], v_new[u]) for u in us]
    kv = [_dot_tn(k[u] * jnp.exp(b_last[u] - b_col[u]), v_new[u]) for u in us]
    for u in us:
        o = qs[u] + av[u]
        s_ref[u] = s[u] * jnp.exp(b_last[u]) + kv[u]
        cols = slice(u * dv, (u + 1) * dv)
        if reverse:
            y = of_ref[:, cols] + o
            o_ref[:, cols] = _head_norm_gate(y, nrm_ref[...], _silu(z_ref[:, cols]), valid).astype(o_ref.dtype)
        else:
            o_ref[:, cols] = o


def _gdn(qkv, gates, dt_row, alog_row, nc, heads, dk, dv, gate_off, of=None, z=None, z_off=0, norm=None):
    reverse = of is not None
    m = qkv.shape[0]
    b = m // (nc * CHUNK)
    unit = _tile(heads, (16, 8, 4, 2, 1))
    ng = heads // unit

    def row(i, c):
        return i * nc + (nc - 1 - c if reverse else c)

    in_specs = [
        pl.BlockSpec((CHUNK, unit * dk), lambda i, g, c: (row(i, c), g)),
        pl.BlockSpec((CHUNK, unit * dk), lambda i, g, c: (row(i, c), ng + g)),
        pl.BlockSpec((CHUNK, unit * dv), lambda i, g, c: (row(i, c), (2 * heads * dk) // (unit * dv) + g)),
        pl.BlockSpec((CHUNK, LANE), lambda i, g, c: (row(i, c), 0)),
        pl.BlockSpec((1, LANE), lambda i, g, c: (0, 0)),
        pl.BlockSpec((1, LANE), lambda i, g, c: (0, 0)),
    ]
    args = [qkv, qkv, qkv, gates, dt_row, alog_row]
    if reverse:
        zb = z_off // (unit * dv)
        in_specs += [
            pl.BlockSpec((CHUNK, unit * dv), lambda i, g, c: (row(i, c), g)),
            pl.BlockSpec((CHUNK, unit * dv), lambda i, g, c: (row(i, c), zb + g)),
            pl.BlockSpec((1, dv), lambda i, g, c: (0, 0)),
        ]
        args += [of, z, norm.reshape(1, dv).astype(F32)]
    return pl.pallas_call(
        functools.partial(_gdn_kernel, unit=unit, heads=heads, dk=dk, dv=dv, gate_off=gate_off, reverse=reverse),
        out_shape=jax.ShapeDtypeStruct((m, heads * dv), BF16 if reverse else F32),
        grid=(b, ng, nc),
        in_specs=in_specs,
        out_specs=pl.BlockSpec((CHUNK, unit * dv), lambda i, g, c: (row(i, c), g)),
        scratch_shapes=[pltpu.VMEM((unit, dk, dv), F32)],
        compiler_params=_cp(("parallel", "parallel", "arbitrary")),
        name="gdn_bwd" if reverse else "gdn_fwd",
    )(*args)


def _gla_kernel(*refs, unit, dk, dv, layer, reverse):
    if reverse:
        q_ref, f_ref, v_ref, lb_ref, of_ref, z_ref, nrm_ref, o_ref, s_ref = refs
    else:
        q_ref, f_ref, v_ref, lb_ref, o_ref, s_ref = refs
    step = pl.program_id(2)
    chunk = pl.num_programs(2) - 1 - step if reverse else step

    @pl.when(step == 0)
    def _():
        s_ref[...] = jnp.zeros_like(s_ref)

    valid = _valid_rows(chunk)
    lbm = lb_ref[...]
    e = jnp.exp(lbm - jnp.max(lbm, axis=0, keepdims=True))
    lb = jnp.sum(e[1:layer + 1], axis=0, keepdims=True) / jnp.sum(e, axis=0, keepdims=True)
    la = jnp.log(lb)
    lc = jnp.log1p(-lb) + _log_sigmoid(f_ref[...])
    logf = jnp.maximum(la, lc) + jnp.log1p(jnp.exp(-jnp.abs(la - lc)))
    key = jnp.where(valid, 1.0 - jnp.exp(logf), 0.0)
    logf = jnp.where(valid, logf, 0.0)
    q = _silu(q_ref[...])
    v = v_ref[...]
    us = range(unit)
    kcol = [slice(u * dk, (u + 1) * dk) for u in us]
    vcol = [slice(u * dv, (u + 1) * dv) for u in us]

    def emit(r0, nrows, u, o):
        rs = slice(r0, r0 + nrows)
        if reverse:
            y = of_ref[rs, vcol[u]] + o
            ok = (chunk * CHUNK + r0 + _iota((nrows, 1), 0)) >= PAD
            o_ref[rs, vcol[u]] = _head_norm_gate(y, nrm_ref[...], _silu(z_ref[rs, vcol[u]]), ok).astype(o_ref.dtype)
        else:
            o_ref[rs, vcol[u]] = o

    cs_all = _scan_rows(logf, CHUNK, reverse)
    in_range = jnp.min(cs_all) >= -GLA_SAFE_DECAY

    @pl.when(in_range)
    def _():
        tri, _, _ = _chunk_masks(reverse)
        last = 0 if reverse else CHUNK - 1
        qe = q * jnp.exp(cs_all)
        kd = key * jnp.exp(-cs_all)
        tot = cs_all[last:last + 1]
        ku = key * jnp.exp(tot - cs_all)
        st = [s_ref[u] for u in us]
        sc = [_dot_nt(qe[:, kcol[u]], kd[:, kcol[u]]) for u in us]
        o_in = [_dot_nt(qe[:, kcol[u]], st[u]) for u in us]
        upd = [_dot_tn(v[:, vcol[u]], ku[:, kcol[u]]) for u in us]
        o_di = [_dot(jnp.where(tri, sc[u], 0.0), v[:, vcol[u]]) for u in us]
        for u in us:
            s_ref[u] = st[u] * jnp.exp(tot[:, kcol[u]]) + upd[u]
            emit(0, CHUNK, u, o_in[u] + o_di[u])

    @pl.when(jnp.logical_not(in_range))
    def _():
        cs = _scan_rows(logf, SUB, reverse)
        qe = q * jnp.exp(cs)
        nsub = CHUNK // SUB
        s16 = _iota((SUB, LANE), 0)
        t16 = _iota((SUB, LANE), 1)
        keep = (s16 >= t16) if reverse else (s16 <= t16)
        last = 0 if reverse else SUB - 1
        order = list(range(nsub - 1, -1, -1) if reverse else range(nsub))
        rows = [slice(i * SUB, (i + 1) * SUB) for i in range(nsub)]
        attn_t = []
        for i in range(nsub):
            cs_i, key_i, q_i = cs[rows[i]], key[rows[i]], q[rows[i]]
            acc = [jnp.zeros((SUB, LANE), F32) for _ in us]
            for t in range(SUB):
                p = (q_i[t:t + 1] * key_i) * jnp.exp(jnp.minimum(cs_i[t:t + 1] - cs_i, 0.0))
                for u in us:
                    acc[u] = jnp.where(t16 == t, jnp.sum(p[:, kcol[u]], axis=1, keepdims=True), acc[u])
            attn_t.append([jnp.where(keep, acc[u], 0.0) for u in us])
        tot = [cs[i * SUB + last:i * SUB + last + 1] for i in range(nsub)]
        kd = [key[rows[i]] * jnp.exp(tot[i] - cs[rows[i]]) for i in range(nsub)]
        st = [s_ref[u] for u in us]
        for i in order:
            o_in = [_dot_nt(qe[rows[i], kcol[u]], st[u]) for u in us]
            o_di = [_dot_tn(attn_t[i][u], v[rows[i], vcol[u]])[:SUB] for u in us]
            upd = [_dot_tn(v[rows[i], vcol[u]], kd[i][:, kcol[u]]) for u in us]
            st = [st[u] * jnp.exp(tot[i][:, kcol[u]]) + upd[u] for u in us]
            for u in us:
                emit(i * SUB, SUB, u, o_in[u] + o_di[u])
        for u in us:
            s_ref[u] = st[u]


def _gla(proj, lower_bounds, nc, heads, dk, dv, layer, of=None, norm=None):
    reverse = of is not None
    m = proj.shape[0]
    b = m // (nc * CHUNK)
    unit = _tile(heads, (8, 4, 2, 1))
    ng = heads // unit
    depth = lower_bounds.shape[0]
    assert dk == dv

    def row(i, c):
        return i * nc + (nc - 1 - c if reverse else c)

    in_specs = [
        pl.BlockSpec((CHUNK, unit * dk), lambda i, g, c: (row(i, c), g)),
        pl.BlockSpec((CHUNK, unit * dk), lambda i, g, c: (row(i, c), (2 if reverse else 1) * ng + g)),
        pl.BlockSpec((CHUNK, unit * dv), lambda i, g, c: (row(i, c), 3 * ng + g)),
        pl.BlockSpec((depth, unit * dk), lambda i, g, c: (0, g)),
    ]
    args = [proj, proj, proj, lower_bounds.astype(F32)]
    if reverse:
        in_specs += [
            pl.BlockSpec((CHUNK, unit * dv), lambda i, g, c: (row(i, c), g)),
            pl.BlockSpec((CHUNK, unit * dv), lambda i, g, c: (row(i, c), 4 * ng + g)),
            pl.BlockSpec((1, dv), lambda i, g, c: (0, 0)),
        ]
        args += [of, proj, norm.reshape(1, dv).astype(F32)]
    return pl.pallas_call(
        functools.partial(_gla_kernel, unit=unit, dk=dk, dv=dv, layer=layer, reverse=reverse),
        out_shape=jax.ShapeDtypeStruct((m, heads * dv), BF16 if reverse else F32),
        grid=(b, ng, nc),
        in_specs=in_specs,
        out_specs=pl.BlockSpec((CHUNK, unit * dv), lambda i, g, c: (row(i, c), g)),
        scratch_shapes=[pltpu.VMEM((unit, dv, dk), F32)],
        compiler_params=_cp(("parallel", "parallel", "arbitrary")),
        name="hgrn2_bwd" if reverse else "hgrn2_fwd",
    )(*args)


def _diff_attn_kernel(q_ref, k_ref, v_ref, qn_ref, kn_ref, lam_ref, sn_ref, o_ref, kn_s, v_s, *, hd, heads, lam_init):
    h = pl.program_id(1)
    qi = pl.program_id(2)
    tq = q_ref.shape[0]
    t = k_ref.shape[0]

    @pl.when(qi == 0)
    def _():
        kk = k_ref[...]
        for mp in range(2):
            km = kk[:, mp * hd:(mp + 1) * hd]
            ms = jnp.mean(km * km, axis=1, keepdims=True)
            kn_s[mp] = (km * lax.rsqrt(ms + EPS) * kn_ref[...]).astype(BF16)
        v_s[...] = v_ref[...].astype(BF16)

    lv = lam_ref[...]
    lam = (jnp.exp(jnp.sum(lv[0:1] * lv[1:2], axis=1, keepdims=True))
           - jnp.exp(jnp.sum(lv[2:3] * lv[3:4], axis=1, keepdims=True)) + lam_init)
    slope2 = jnp.exp2(-8.0 * (jnp.full((1, 1), h + 1, jnp.int32).astype(F32)) / heads) * LOG2E
    kpos = _iota((1, t), 1)
    kbias = jnp.where(kpos >= PAD, slope2 * kpos.astype(F32), NEG)
    rb = _tile(tq, (88, 64, 32, 16, 8))
    blocks = range(0, tq, rb)
    ridx = [qi * tq + r0 + _iota((rb, 1), 0) for r0 in blocks]
    nb = len(ridx)
    s = []
    for r0 in blocks:
        q = q_ref[r0:r0 + rb, :]
        for mp in range(2):
            qm = q[:, mp * hd:(mp + 1) * hd]
            ms = jnp.mean(qm * qm, axis=1, keepdims=True)
            qn = qm * lax.rsqrt(ms + EPS) * qn_ref[...]
            s.append(lax.dot_general(qn.astype(BF16), kn_s[mp], (((1,), (1,)), ((), ())),
                                     preferred_element_type=F32))
    a = []
    for i in range(nb):
        bias2 = jnp.abs(slope2 * ridx[i].astype(F32) - kbias)
        es, inv = [], []
        for mp in range(2):
            x = s[2 * i + mp] * (hd ** -0.5 * LOG2E) - bias2
            e = jnp.exp2(x - jnp.max(x, axis=1, keepdims=True))
            es.append(e)
            inv.append(1.0 / jnp.sum(e, axis=1, keepdims=True))
        a.append((es[0] * inv[0] - es[1] * (lam * inv[1])).astype(BF16))
    o = [jnp.dot(a[i], v_s[...], preferred_element_type=F32) for i in range(nb)]
    for i, r0 in enumerate(blocks):
        ms = jnp.mean(o[i] * o[i], axis=1, keepdims=True)
        y = o[i] * lax.rsqrt(ms + EPS) * sn_ref[...] * (1.0 - lam_init)
        o_ref[r0:r0 + rb, :] = jnp.where(ridx[i] >= PAD, y, 0.0).astype(o_ref.dtype)


def _diff_attn(proj, col_off, t, heads, hd, q_norm, k_norm, lam_vec, sub_norm, lam_init):
    m = proj.shape[0]
    b = m // t
    w = 2 * hd
    tq = _tile(t, (264, 192, 64))
    nq = t // tq
    c0 = col_off // w
    return pl.pallas_call(
        functools.partial(_diff_attn_kernel, hd=hd, heads=heads, lam_init=lam_init),
        out_shape=jax.ShapeDtypeStruct((m, heads * w), BF16),
        grid=(b, heads, nq),
        in_specs=[
            pl.BlockSpec((tq, w), lambda i, h, j: (i * nq + j, c0 + h)),
            pl.BlockSpec((t, w), lambda i, h, j: (i, c0 + heads + h)),
            pl.BlockSpec((t, w), lambda i, h, j: (i, c0 + 2 * heads + h)),
            pl.BlockSpec((1, hd), lambda i, h, j: (0, 0)),
            pl.BlockSpec((1, hd), lambda i, h, j: (0, 0)),
            pl.BlockSpec((4, hd), lambda i, h, j: (0, 0)),
            pl.BlockSpec((1, w), lambda i, h, j: (0, 0)),
        ],
        out_specs=pl.BlockSpec((tq, w), lambda i, h, j: (i * nq + j, h)),
        scratch_shapes=[pltpu.VMEM((2, t, hd), BF16), pltpu.VMEM((t, w), BF16)],
        compiler_params=_cp(("parallel", "parallel", "arbitrary")),
        name="diff_attn",
    )(proj, proj, proj, q_norm.reshape(1, hd).astype(F32), k_norm.reshape(1, hd).astype(F32),
      lam_vec.astype(F32), sub_norm.reshape(1, w).astype(F32))


def _place(width, pieces):
    row = jnp.zeros((width,), F32)
    for off, vals in pieces:
        row = lax.dynamic_update_slice(row, vals.astype(F32).reshape(-1), (off,))
    return row.reshape(1, width)


def _trunk(h, t, p):
    m, d = h.shape
    nc = t // CHUNK
    mix_w = d // 2
    depth = p["norm_mix"].shape[0]
    ha = p["mlstm_i_bias"].shape[-1]
    a_dv = mix_w // ha
    a_dqk = a_dv // 2
    hb = p["dn_a_log"].shape[-1]
    b_dk = mix_w // hb
    c_dv = p["hgrn_norm"].shape[-1]
    hc = mix_w // c_dv
    c_dk = p["hgrn_lower_bounds"].shape[-1] // hc
    d_hd = p["diff_q_norm"].shape[-1]
    hdh = mix_w // (2 * d_hd)
    o_ag = 2 * ha * a_dqk + 2 * ha * a_dv
    o_qkv = o_ag + 4 * ha
    o_bz = o_qkv + 3 * mix_w
    o_bg = o_bz + mix_w
    assert 4 * ha + 4 * hb <= LANE
    for l in range(depth):
        hn = _rmsnorm_bf16(h, p["norm_mix"][l])
        if l % 2 == 0:
            e = l // 2
            w_in = p["ab_w_in"][e]
            w_plain = jnp.concatenate([w_in[:, :o_ag], w_in[:, o_bz:o_bg]], axis=1).astype(BF16)
            w_gate = jnp.concatenate(
                [w_in[:, o_ag:o_qkv], w_in[:, o_bg:], jnp.zeros((d, LANE - 4 * ha - 4 * hb), F32)], axis=1).astype(BF16)
            w_qkv = w_in[:, o_qkv:o_bz].astype(BF16)
            proj = _matmul(hn, w_plain, name="ab_proj")
            gates = _matmul(hn, w_gate, name="ab_gates")
            qkv = _qkv_conv(hn, w_qkv, p["dn_conv"][e], t, b_dk)
            bias_row = _place(LANE, [(dd * 2 * ha + gg * ha, (p["mlstm_i_bias"][e], p["mlstm_f_bias"][e])[gg][dd])
                                     for dd in range(2) for gg in range(2)])
            yf = _mlstm(proj, gates, bias_row, nc, ha, a_dqk, a_dv)
            xa = _mlstm(proj, gates, bias_row, nc, ha, a_dqk, a_dv, yf=yf, norm=p["mlstm_norm"][e])
            dt_row = _place(LANE, [(4 * ha + dd * 2 * hb, p["dn_dt_bias"][e][dd]) for dd in range(2)])
            alog_row = _place(LANE, [(4 * ha + dd * 2 * hb, p["dn_a_log"][e][dd]) for dd in range(2)])
            of = _gdn(qkv, gates, dt_row, alog_row, nc, hb, b_dk, b_dk, 4 * ha)
            xb = _gdn(qkv, gates, dt_row, alog_row, nc, hb, b_dk, b_dk, 4 * ha, of=of, z=proj, z_off=o_ag,
                      norm=p["dn_norm"][e])
            w_out = p["ab_w_out"][e].astype(BF16)
        else:
            o = l // 2
            proj = _matmul(hn, p["cd_w_in"][o].astype(BF16), name="cd_proj")
            of = _gla(proj, p["hgrn_lower_bounds"], nc, hc, c_dk, c_dv, l)
            xa = _gla(proj, p["hgrn_lower_bounds"], nc, hc, c_dk, c_dv, l, of=of, norm=p["hgrn_norm"][o])
            lam_init = 0.8 - 0.6 * math.exp(-0.3 * l)
            xb = _diff_attn(proj, 3 * hc * c_dk + 2 * hc * c_dv, t, hdh, d_hd, p["diff_q_norm"][o],
                            p["diff_k_norm"][o], p["diff_lambda"][o], p["diff_norm"][o], lam_init)
            w_out = p["cd_w_out"][o].astype(BF16)
        h = _matmul2_residual(xa, xb, w_out, h)
        hn = _rmsnorm_bf16(h, p["norm_ffn"][l])
        act = _ffn_up(hn, p["ffn_w_up"][l].astype(BF16), p["ffn_conv_w"][l], p["ffn_conv_b"][l], t)
        h = _matmul_residual(act, p["ffn_w_down"][l].astype(BF16), h)
    return h


def kernel(x_prompt, x_sample, meta_tokens, norm_mix, norm_ffn, ab_w_in, ab_w_out, mlstm_i_bias, mlstm_f_bias, mlstm_norm, dn_conv, dn_a_log, dn_dt_bias, dn_norm, cd_w_in, cd_w_out, hgrn_lower_bounds, hgrn_norm, diff_q_norm, diff_k_norm, diff_lambda, diff_norm, ffn_w_up, ffn_conv_w, ffn_conv_b, ffn_w_down):
    p = dict(norm_mix=norm_mix, norm_ffn=norm_ffn, ab_w_in=ab_w_in, ab_w_out=ab_w_out, mlstm_i_bias=mlstm_i_bias,
             mlstm_f_bias=mlstm_f_bias, mlstm_norm=mlstm_norm, dn_conv=dn_conv, dn_a_log=dn_a_log,
             dn_dt_bias=dn_dt_bias, dn_norm=dn_norm, cd_w_in=cd_w_in, cd_w_out=cd_w_out,
             hgrn_lower_bounds=hgrn_lower_bounds, hgrn_norm=hgrn_norm, diff_q_norm=diff_q_norm,
             diff_k_norm=diff_k_norm, diff_lambda=diff_lambda, diff_norm=diff_norm, ffn_w_up=ffn_w_up,
             ffn_conv_w=ffn_conv_w, ffn_conv_b=ffn_conv_b, ffn_w_down=ffn_w_down)
    bp, s, d = x_prompt.shape
    bs = x_sample.shape[0]
    assert x_sample.shape[1:] == (s, d)
    t = PAD + N_META + s
    assert t % CHUNK == 0
    x = jnp.concatenate([x_prompt, x_sample], axis=0).astype(F32)
    b = bp + bs
    front = jnp.concatenate([jnp.zeros((PAD, d), F32), meta_tokens.astype(F32)], axis=0)
    h = jnp.concatenate([jnp.broadcast_to(front[None], (b, PAD + N_META, d)), x], axis=1).reshape(b * t, d)
    h = _trunk(h, t, p)
    y = h.reshape(b, t, d)[:, PAD + N_META:]
    return (y[:bp], y[bp:])
```

```python
import functools
import math

import jax
import jax.numpy as jnp
from jax import lax
from jax.experimental import pallas as pl
from jax.experimental.pallas import tpu as pltpu

F32 = jnp.float32
BF16 = jnp.bfloat16
EPS = 1e-6
NEG = -1e30
N_META = 16
CHUNK = 64
PAD = CHUNK - N_META
SUB = 16
GLA_SAFE_DECAY = 60.0
LANE = 128
LOG2E = 1.4426950408889634
VMEM_LIMIT = 56 * 1024 * 1024


def _cp(sem, vmem=VMEM_LIMIT):
    return pltpu.CompilerParams(dimension_semantics=sem, vmem_limit_bytes=vmem)


def _iota(shape, dim):
    return lax.broadcasted_iota(jnp.int32, shape, dim)


def _dot(a, b):
    return jnp.dot(a.astype(BF16), b.astype(BF16), preferred_element_type=F32)


def _dot_nt(a, b):
    return lax.dot_general(a.astype(BF16), b.astype(BF16), (((1,), (1,)), ((), ())), preferred_element_type=F32)


def _dot_tn(a, b):
    return lax.dot_general(a.astype(BF16), b.astype(BF16), (((0,), (0,)), ((), ())), preferred_element_type=F32)


def _log_sigmoid(x):
    return jnp.minimum(x, 0.0) - jnp.log1p(jnp.exp(-jnp.abs(x)))


def _softplus(x):
    return jnp.maximum(x, 0.0) + jnp.log1p(jnp.exp(-jnp.abs(x)))


def _sigmoid(x):
    return 1.0 / (1.0 + jnp.exp(-x))


def _silu(x):
    return x * _sigmoid(x)


def _tile(n, cands):
    for c in cands:
        if n % c == 0:
            return c
    return n


def _rms_kernel(x_ref, g_ref, o_ref):
    x = x_ref[...]
    ms = jnp.mean(x * x, axis=-1, keepdims=True)
    o_ref[...] = (x * lax.rsqrt(ms + EPS) * g_ref[...]).astype(o_ref.dtype)


def _rmsnorm_bf16(x, g):
    m, d = x.shape
    tm = _tile(m, (528, 264, 192, 64))
    return pl.pallas_call(
        _rms_kernel,
        out_shape=jax.ShapeDtypeStruct((m, d), BF16),
        grid=(m // tm,),
        in_specs=[pl.BlockSpec((tm, d), lambda i: (i, 0)), pl.BlockSpec((1, d), lambda i: (0, 0))],
        out_specs=pl.BlockSpec((tm, d), lambda i: (i, 0)),
        compiler_params=_cp(("parallel",)),
        name="rmsnorm",
    )(x, g.reshape(1, d).astype(F32))


def _mm_kernel(x_ref, w_ref, o_ref):
    o_ref[...] = jnp.dot(x_ref[...], w_ref[...], preferred_element_type=F32).astype(o_ref.dtype)


def _matmul(x, w, out_dtype=F32, name="matmul"):
    m, k = x.shape
    n = w.shape[1]
    tm = _tile(m, (1056, 704, 192, 64))
    tn = _tile(n, (512, 256, 128))
    return pl.pallas_call(
        _mm_kernel,
        out_shape=jax.ShapeDtypeStruct((m, n), out_dtype),
        grid=(m // tm, n // tn),
        in_specs=[pl.BlockSpec((tm, k), lambda i, j: (i, 0)), pl.BlockSpec((k, tn), lambda i, j: (0, j))],
        out_specs=pl.BlockSpec((tm, tn), lambda i, j: (i, j)),
        compiler_params=_cp(("parallel", "arbitrary")),
        name=name,
    )(x, w)


def _mm2_res_kernel(xa_ref, xb_ref, wa_ref, wb_ref, r_ref, o_ref):
    acc = jnp.dot(xa_ref[...], wa_ref[...], preferred_element_type=F32)
    acc = acc + jnp.dot(xb_ref[...], wb_ref[...], preferred_element_type=F32)
    o_ref[...] = r_ref[...] + acc


def _matmul2_residual(xa, xb, w, res, name="mix_out"):
    m, ka = xa.shape
    kb = xb.shape[1]
    n = w.shape[1]
    tm = _tile(m, (1056, 704, 192, 64))
    tn = _tile(n, (512, 256, 128))
    assert ka == kb
    return pl.pallas_call(
        _mm2_res_kernel,
        out_shape=jax.ShapeDtypeStruct((m, n), F32),
        grid=(m // tm, n // tn),
        in_specs=[
            pl.BlockSpec((tm, ka), lambda i, j: (i, 0)),
            pl.BlockSpec((tm, kb), lambda i, j: (i, 0)),
            pl.BlockSpec((ka, tn), lambda i, j: (0, j)),
            pl.BlockSpec((kb, tn), lambda i, j: (1, j)),
            pl.BlockSpec((tm, tn), lambda i, j: (i, j)),
        ],
        out_specs=pl.BlockSpec((tm, tn), lambda i, j: (i, j)),
        compiler_params=_cp(("parallel", "arbitrary")),
        name=name,
    )(xa, xb, w, w, res)


def _mm_res_kernel(x_ref, w_ref, r_ref, o_ref):
    o_ref[...] = r_ref[...] + jnp.dot(x_ref[...], w_ref[...], preferred_element_type=F32)


def _matmul_residual(x, w, layer, res, name="ffn_down"):
    m, k = x.shape
    n = w.shape[2]
    tm = _tile(m, (1056, 704, 192, 64))
    tn = _tile(n, (256, 128))
    return pl.pallas_call(
        _mm_res_kernel,
        out_shape=jax.ShapeDtypeStruct((m, n), F32),
        grid=(m // tm, n // tn),
        in_specs=[
            pl.BlockSpec((tm, k), lambda i, j: (i, 0), pipeline_mode=pl.Buffered(1)),
            pl.BlockSpec((None, k, tn), lambda i, j: (layer, 0, j)),
            pl.BlockSpec((tm, tn), lambda i, j: (i, j)),
        ],
        out_specs=pl.BlockSpec((tm, tn), lambda i, j: (i, j)),
        compiler_params=_cp(("parallel", "arbitrary")),
        name=name,
    )(x, w, res)


def _matmul_residual_tokens(x, w, layer, res, t, seq0, nseq, name="ffn_down_out"):
    m, k = x.shape
    n = w.shape[2]
    s = t - PAD - N_META
    tm = _tile(s, (1024, 512, 128, 64))
    per = s // tm
    tn = _tile(n, (256, 128))

    def row0(a, r):
        return pl.multiple_of((seq0 + a) * t + (PAD + N_META) + r * tm, CHUNK)

    return pl.pallas_call(
        _mm_res_kernel,
        out_shape=jax.ShapeDtypeStruct((nseq * s, n), F32),
        grid=(nseq, per, n // tn),
        in_specs=[
            pl.BlockSpec((pl.Element(tm), pl.Element(k)), lambda a, r, j: (row0(a, r), 0),
                         pipeline_mode=pl.Buffered(1)),
            pl.BlockSpec((None, k, tn), lambda a, r, j: (layer, 0, j)),
            pl.BlockSpec((pl.Element(tm), pl.Element(tn)), lambda a, r, j: (row0(a, r), j * tn)),
        ],
        out_specs=pl.BlockSpec((tm, tn), lambda a, r, j: (a * per + r, j)),
        compiler_params=_cp(("parallel", "parallel", "arbitrary")),
        name=name,
    )(x, w, res)


def _conv3_rows(y, cw):
    t = y.shape[0]
    return cw[0:1] * pltpu.roll(y, 1, 0) + cw[1:2] * y + cw[2:3] * pltpu.roll(y, t - 1, 0)


def _ffn_up_kernel(x_ref, wg_ref, wv_ref, cw_ref, cb_ref, o_ref):
    x = x_ref[...]
    g = jnp.dot(x, wg_ref[...], preferred_element_type=F32)
    v = jnp.dot(x, wv_ref[...], preferred_element_type=F32)
    gc = _conv3_rows(g, cw_ref[...]) + cb_ref[...]
    o_ref[...] = (_silu(gc) * v).astype(o_ref.dtype)


def _ffn_up(xn, w_up, layer, conv_w, conv_b, t):
    m, d = xn.shape
    f = conv_w.shape[1]
    tn = _tile(f, (256, 128))
    nj = f // tn
    return pl.pallas_call(
        _ffn_up_kernel,
        out_shape=jax.ShapeDtypeStruct((m, f), BF16),
        grid=(m // t, nj),
        in_specs=[
            pl.BlockSpec((t, d), lambda i, j: (i, 0), pipeline_mode=pl.Buffered(1)),
            pl.BlockSpec((None, d, tn), lambda i, j: (layer, 0, j)),
            pl.BlockSpec((None, d, tn), lambda i, j: (layer, 0, j + nj)),
            pl.BlockSpec((3, tn), lambda i, j: (0, j)),
            pl.BlockSpec((1, tn), lambda i, j: (0, j)),
        ],
        out_specs=pl.BlockSpec((t, tn), lambda i, j: (i, j)),
        compiler_params=_cp(("parallel", "arbitrary")),
        name="ffn_up",
    )(xn, w_up, w_up, conv_w.astype(F32), conv_b.reshape(1, f).astype(F32))


def _qkv_conv_kernel(x_ref, w_ref, cw_ref, o_ref, *, nq, qscale):
    j = pl.program_id(1)
    y = jnp.dot(x_ref[...], w_ref[...], preferred_element_type=F32)
    a = _silu(_conv3_rows(y, cw_ref[...]))
    t, tn = a.shape
    a = jnp.where(_iota((t, 1), 0) >= PAD, a, 0.0)
    is_qk = j < 2 * nq
    scale = jnp.where(j < nq, qscale, 1.0)
    for s in range(tn // LANE):
        seg = a[:, s * LANE:(s + 1) * LANE]
        ss = jnp.sum(seg * seg, axis=1, keepdims=True)
        nrm = seg * lax.rsqrt(ss + EPS) * scale
        o_ref[:, s * LANE:(s + 1) * LANE] = jnp.where(is_qk, nrm, seg)


def _qkv_conv(xn, w, conv_w, t, dk):
    m, d = xn.shape
    n = w.shape[1]
    tn = _tile(n // 3, (256, 128))
    return pl.pallas_call(
        functools.partial(_qkv_conv_kernel, nq=(n // 3) // tn, qscale=dk ** -0.5),
        out_shape=jax.ShapeDtypeStruct((m, n), F32),
        grid=(m // t, n // tn),
        in_specs=[
            pl.BlockSpec((t, d), lambda i, j: (i, 0), pipeline_mode=pl.Buffered(1)),
            pl.BlockSpec((d, tn), lambda i, j: (0, j)),
            pl.BlockSpec((3, tn), lambda i, j: (0, j)),
        ],
        out_specs=pl.BlockSpec((t, tn), lambda i, j: (i, j)),
        compiler_params=_cp(("parallel", "arbitrary")),
        name="dn_qkv_conv",
    )(xn, w, conv_w.astype(F32))


def _chunk_masks(reverse):
    r = _iota((CHUNK, CHUNK), 0)
    s = _iota((CHUNK, CHUNK), 1)
    tri = (s >= r) if reverse else (s <= r)
    strict = (s > r) if reverse else (s < r)
    return tri, strict, r == s


def _scan_rows(x, seg, reverse):
    rows = x.shape[0]
    r = _iota((rows, 1), 0) % seg
    s = 1
    while s < seg:
        if reverse:
            x = x + jnp.where(r < seg - s, pltpu.roll(x, rows - s, 0), 0.0)
        else:
            x = x + jnp.where(r >= s, pltpu.roll(x, s, 0), 0.0)
        s *= 2
    return x


def _transpose_tile(x):
    return jnp.concatenate([x, jnp.zeros((LANE - x.shape[0], LANE), F32)], axis=0).T


def _valid_rows(chunk):
    return (chunk * CHUNK + _iota((CHUNK, 1), 0)) >= PAD


def _head_norm_gate(y, gain, gate, valid):
    ms = jnp.mean(y * y, axis=1, keepdims=True)
    return jnp.where(valid, y * lax.rsqrt(ms + EPS) * gain * gate, 0.0)


def _mlstm_kernel(*refs, nseq, heads, dk, dv, reverse):
    if reverse:
        q_ref, k_ref, v_ref, g_ref, bias_ref, yf_ref, ao_ref, nrm_ref, o_ref, c_ref, n_ref, m_ref = refs
    else:
        q_ref, k_ref, v_ref, g_ref, bias_ref, o_ref, c_ref, n_ref, m_ref = refs
    step = pl.program_id(1)
    chunk = pl.num_programs(1) - 1 - step if reverse else step

    @pl.when(step == 0)
    def _():
        c_ref[...] = jnp.zeros_like(c_ref)
        n_ref[...] = jnp.zeros_like(n_ref)
        m_ref[...] = jnp.full_like(m_ref, NEG)

    valid = _valid_rows(chunk)
    tri, _, _ = _chunk_masks(reverse)
    last = 0 if reverse else CHUNK - 1
    base = 2 * heads if reverse else 0
    i_all, cum, i_t, cum_t = [], [], [], []
    for sq in range(nseq):
        gt = g_ref[sq] + bias_ref[...]
        i_all.append(jnp.where(valid, gt, NEG))
        cum.append(_scan_rows(jnp.where(valid, _log_sigmoid(gt), 0.0), CHUNK, reverse))
        i_t.append(_transpose_tile(i_all[sq]))
        cum_t.append(_transpose_tile(cum[sq]))
    units = [(sq, h) for sq in range(nseq) for h in range(heads)]
    us = range(len(units))
    ci = [base + h for _, h in units]
    cf = [base + heads + h for _, h in units]
    i_col = [i_all[sq][:, ci[u]:ci[u] + 1] for u, (sq, _) in enumerate(units)]
    i_row = [i_t[sq][ci[u]:ci[u] + 1, :CHUNK] for u, (sq, _) in enumerate(units)]
    b_col = [cum[sq][:, cf[u]:cf[u] + 1] for u, (sq, _) in enumerate(units)]
    b_row = [cum_t[sq][cf[u]:cf[u] + 1, :CHUNK] for u, (sq, _) in enumerate(units)]
    b_last = [cum[sq][last:last + 1, cf[u]:cf[u] + 1] for u, (sq, _) in enumerate(units)]
    q = [q_ref[sq, :, h * dk:(h + 1) * dk] for sq, h in units]
    k = [k_ref[sq, :, h * dk:(h + 1) * dk] * (dk ** -0.5) for sq, h in units]
    v = [v_ref[sq, :, h * dv:(h + 1) * dv] for sq, h in units]
    ct = [c_ref[u] for u in us]
    n_row = [n_ref[u] for u in us]
    m = [m_ref[u][:, 0:1] for u in us]
    qk = [_dot_nt(q[u], k[u]) for u in us]
    qc = [_dot(q[u], ct[u]) for u in us]
    dmat = [jnp.where(tri, b_col[u] - b_row[u] + i_row[u], NEG) for u in us]
    inter = [b_col[u] + m[u] for u in us]
    m_t = [jnp.maximum(inter[u], jnp.max(dmat[u], axis=1, keepdims=True)) for u in us]
    w = [jnp.exp(dmat[u] - m_t[u]) * qk[u] for u in us]
    s_inter = [jnp.exp(inter[u] - m_t[u]) for u in us]
    wv = [_dot(w[u], v[u]) for u in us]
    dec = [b_last[u] - b_col[u] + i_col[u] for u in us]
    m_new = [jnp.maximum(b_last[u] + m[u], jnp.max(dec[u], axis=0, keepdims=True)) for u in us]
    a_old = [jnp.exp(b_last[u] + m[u] - m_new[u]) for u in us]
    w_in = [jnp.exp(dec[u] - m_new[u]) for u in us]
    kv = [_dot_tn(k[u], w_in[u] * v[u]) for u in us]
    for u, (sq, h) in enumerate(units):
        num = s_inter[u] * qc[u] + wv[u]
        den = (s_inter[u] * jnp.sum(q[u] * n_row[u], axis=1, keepdims=True)
               + jnp.sum(w[u], axis=1, keepdims=True))
        hout = num / jnp.maximum(jnp.abs(den), jnp.exp(-m_t[u]))
        c_ref[u] = a_old[u] * ct[u] + kv[u]
        n_ref[u] = a_old[u] * n_row[u] + jnp.sum(w_in[u] * k[u], axis=0, keepdims=True)
        m_ref[u] = jnp.broadcast_to(m_new[u], (1, LANE))
        cols = slice(h * dv, (h + 1) * dv)
        if reverse:
            y = yf_ref[sq, :, cols] + hout
            o_ref[sq, :, cols] = _head_norm_gate(
                y, nrm_ref[:, cols], _sigmoid(ao_ref[sq, :, cols]), valid).astype(o_ref.dtype)
        else:
            o_ref[sq, :, cols] = hout


def _mlstm(proj, gates, bias_row, nc, heads, dk, dv, yf=None, norm=None):
    reverse = yf is not None
    m = proj.shape[0]
    t = nc * CHUNK
    b = m // t
    wq, wv = heads * dk, heads * dv
    nseq = _tile(b, (3, 2, 1))
    proj3 = proj.reshape(b, t, proj.shape[1])
    gates3 = gates.reshape(b, t, LANE)

    def ch(c):
        return nc - 1 - c if reverse else c

    in_specs = [
        pl.BlockSpec((nseq, CHUNK, wq), lambda i, c: (i, ch(c), 0)),
        pl.BlockSpec((nseq, CHUNK, wq), lambda i, c: (i, ch(c), 1)),
        pl.BlockSpec((nseq, CHUNK, wv), lambda i, c: (i, ch(c), (2 * wq) // wv)),
        pl.BlockSpec((nseq, CHUNK, LANE), lambda i, c: (i, ch(c), 0)),
        pl.BlockSpec((1, LANE), lambda i, c: (0, 0)),
    ]
    args = [proj3, proj3, proj3, gates3, bias_row]
    if reverse:
        in_specs += [
            pl.BlockSpec((nseq, CHUNK, wv), lambda i, c: (i, ch(c), 0)),
            pl.BlockSpec((nseq, CHUNK, wv), lambda i, c: (i, ch(c), (2 * wq) // wv + 1)),
            pl.BlockSpec((1, wv), lambda i, c: (0, 0)),
        ]
        args += [yf.reshape(b, t, wv), proj3, norm.reshape(1, wv).astype(F32)]
    out = pl.pallas_call(
        functools.partial(_mlstm_kernel, nseq=nseq, heads=heads, dk=dk, dv=dv, reverse=reverse),
        out_shape=jax.ShapeDtypeStruct((b, t, wv), BF16 if reverse else F32),
        grid=(b // nseq, nc),
        in_specs=in_specs,
        out_specs=pl.BlockSpec((nseq, CHUNK, wv), lambda i, c: (i, ch(c), 0)),
        scratch_shapes=[
            pltpu.VMEM((nseq * heads, dk, dv), F32),
            pltpu.VMEM((nseq * heads, 1, dk), F32),
            pltpu.VMEM((nseq * heads, 1, LANE), F32),
        ],
        compiler_params=_cp(("parallel", "arbitrary")),
        name="mlstm_bwd" if reverse else "mlstm_fwd",
    )(*args)
    return out.reshape(m, wv)


def _gdn_kernel(*refs, nseq, heads, dk, dv, gate_off, reverse):
    if reverse:
        q_ref, k_ref, v_ref, g_ref, dt_ref, alog_ref, of_ref, z_ref, nrm_ref, o_ref, s_ref = refs
    else:
        q_ref, k_ref, v_ref, g_ref, dt_ref, alog_ref, o_ref, s_ref = refs
    step = pl.program_id(1)
    chunk = pl.num_programs(1) - 1 - step if reverse else step

    @pl.when(step == 0)
    def _():
        s_ref[...] = jnp.zeros_like(s_ref)

    valid = _valid_rows(chunk)
    tri, strict, eye = _chunk_masks(reverse)
    eye_f = jnp.where(eye, 1.0, 0.0)
    last = 0 if reverse else CHUNK - 1
    base = gate_off + (2 * heads if reverse else 0)
    cum, cum_t, beta_all = [], [], []
    for sq in range(nseq):
        raw = g_ref[sq]
        decay = jnp.where(valid, -jnp.exp(alog_ref[...]) * _softplus(raw + dt_ref[...]), 0.0)
        beta_all.append(jnp.where(valid, _sigmoid(raw), 0.0))
        cum.append(_scan_rows(decay, CHUNK, reverse))
        cum_t.append(_transpose_tile(cum[sq]))
    units = [(sq, h) for sq in range(nseq) for h in range(heads)]
    us = range(len(units))
    b_col = [cum[sq][:, base + h:base + h + 1] for sq, h in units]
    b_row = [cum_t[sq][base + h:base + h + 1, :CHUNK] for sq, h in units]
    b_last = [cum[sq][last:last + 1, base + h:base + h + 1] for sq, h in units]
    beta = [beta_all[sq][:, base + heads + h:base + heads + h + 1] for sq, h in units]
    q = [q_ref[sq, :, h * dk:(h + 1) * dk] for sq, h in units]
    k = [k_ref[sq, :, h * dk:(h + 1) * dk] for sq, h in units]
    v = [v_ref[sq, :, h * dv:(h + 1) * dv] for sq, h in units]
    s = [s_ref[u] for u in us]
    kb = [k[u] * beta[u] for u in us]
    kk = [_dot_nt(kb[u], k[u]) for u in us]
    qk = [_dot_nt(q[u], k[u]) for u in us]
    decay = [jnp.exp(jnp.where(tri, b_col[u] - b_row[u], NEG)) for u in us]
    eb = [jnp.exp(b_col[u]) for u in us]
    qs = [_dot(q[u] * eb[u], s[u]) for u in us]
    x = [jnp.where(strict, kk[u] * decay[u], 0.0) for u in us]
    rhs = [jnp.concatenate([v[u] * beta[u], kb[u] * eb[u]], axis=1) for u in us]
    p = [eye_f - x[u] for u in us]
    n = 1
    while 2 * n < CHUNK:
        x = [_dot(x[u], x[u]) for u in us]
        p = [p[u] + _dot(p[u], x[u]) for u in us]
        n *= 2
    sol = [_dot(p[u], rhs[u]) for u in us]
    ws = [_dot(sol[u][:, dv:], s[u]) for u in us]
    v_new = [sol[u][:, :dv] - ws[u] for u in us]
    attn = [qk[u] * decay[u] for u in us]
    av = [_dot(attn[u], v_new[u]) for u in us]
    kv = [_dot_tn(k[u] * jnp.exp(b_last[u] - b_col[u]), v_new[u]) for u in us]
    for u, (sq, h) in enumerate(units):
        o = qs[u] + av[u]
        s_ref[u] = s[u] * jnp.exp(b_last[u]) + kv[u]
        cols = slice(h * dv, (h + 1) * dv)
        if reverse:
            y = of_ref[sq, :, cols] + o
            o_ref[sq, :, cols] = _head_norm_gate(
                y, nrm_ref[...], _silu(z_ref[sq, :, cols]), valid).astype(o_ref.dtype)
        else:
            o_ref[sq, :, cols] = o


def _gdn(qkv, gates, dt_row, alog_row, nc, heads, dk, dv, gate_off, of=None, z=None, z_off=0, norm=None):
    reverse = of is not None
    m = qkv.shape[0]
    t = nc * CHUNK
    b = m // t
    wk, wv = heads * dk, heads * dv
    nseq = _tile(b, (2, 1))
    qkv3 = qkv.reshape(b, t, qkv.shape[1])

    def ch(c):
        return nc - 1 - c if reverse else c

    in_specs = [
        pl.BlockSpec((nseq, CHUNK, wk), lambda i, c: (i, ch(c), 0)),
        pl.BlockSpec((nseq, CHUNK, wk), lambda i, c: (i, ch(c), 1)),
        pl.BlockSpec((nseq, CHUNK, wv), lambda i, c: (i, ch(c), (2 * wk) // wv)),
        pl.BlockSpec((nseq, CHUNK, LANE), lambda i, c: (i, ch(c), 0)),
        pl.BlockSpec((1, LANE), lambda i, c: (0, 0)),
        pl.BlockSpec((1, LANE), lambda i, c: (0, 0)),
    ]
    args = [qkv3, qkv3, qkv3, gates.reshape(b, t, LANE), dt_row, alog_row]
    if reverse:
        in_specs += [
            pl.BlockSpec((nseq, CHUNK, wv), lambda i, c: (i, ch(c), 0)),
            pl.BlockSpec((nseq, CHUNK, wv), lambda i, c: (i, ch(c), z_off // wv)),
            pl.BlockSpec((1, dv), lambda i, c: (0, 0)),
        ]
        args += [of.reshape(b, t, wv), z.reshape(b, t, z.shape[1]), norm.reshape(1, dv).astype(F32)]
    out = pl.pallas_call(
        functools.partial(_gdn_kernel, nseq=nseq, heads=heads, dk=dk, dv=dv, gate_off=gate_off, reverse=reverse),
        out_shape=jax.ShapeDtypeStruct((b, t, wv), BF16 if reverse else F32),
        grid=(b // nseq, nc),
        in_specs=in_specs,
        out_specs=pl.BlockSpec((nseq, CHUNK, wv), lambda i, c: (i, ch(c), 0)),
        scratch_shapes=[pltpu.VMEM((nseq * heads, dk, dv), F32)],
        compiler_params=_cp(("parallel", "arbitrary")),
        name="gdn_bwd" if reverse else "gdn_fwd",
    )(*args)
    return out.reshape(m, wv)


def _gla_kernel(*refs, unit, dk, dv, layer, reverse):
    if reverse:
        q_ref, f_ref, v_ref, lb_ref, of_ref, z_ref, nrm_ref, o_ref, s_ref = refs
    else:
        q_ref, f_ref, v_ref, lb_ref, o_ref, s_ref = refs
    step = pl.program_id(2)
    chunk = pl.num_programs(2) - 1 - step if reverse else step

    @pl.when(step == 0)
    def _():
        s_ref[...] = jnp.zeros_like(s_ref)

    valid = _valid_rows(chunk)
    lbm = lb_ref[...]
    e = jnp.exp(lbm - jnp.max(lbm, axis=0, keepdims=True))
    lb = jnp.sum(e[1:layer + 1], axis=0, keepdims=True) / jnp.sum(e, axis=0, keepdims=True)
    us = range(unit)
    kcol = [slice(u * dk, (u + 1) * dk) for u in us]
    vcol = [slice(u * dv, (u + 1) * dv) for u in us]
    logf, key, q, cs_all = [], [], [], []
    for u in us:
        x = f_ref[:, kcol[u]]
        lbu = lb[:, kcol[u]]
        z = jnp.exp(-jnp.abs(x))
        r = 1.0 / (1.0 + z)
        f = lbu + (1.0 - lbu) * (jnp.where(x >= 0.0, 1.0, z) * r)
        key.append(jnp.where(valid, (1.0 - lbu) * (jnp.where(x >= 0.0, z, 1.0) * r), 0.0))
        logf.append(jnp.where(valid, jnp.where(f > 0.0, jnp.log(f), x), 0.0))
        q.append(_silu(q_ref[:, kcol[u]]))
        cs_all.append(_scan_rows(logf[u], CHUNK, reverse))
    v = v_ref[...]

    def emit(r0, nrows, u, o):
        rs = slice(r0, r0 + nrows)
        if reverse:
            y = of_ref[rs, vcol[u]] + o
            ok = (chunk * CHUNK + r0 + _iota((nrows, 1), 0)) >= PAD
            o_ref[rs, vcol[u]] = _head_norm_gate(y, nrm_ref[...], _silu(z_ref[rs, vcol[u]]), ok).astype(o_ref.dtype)
        else:
            o_ref[rs, vcol[u]] = o

    lowest = jnp.min(cs_all[0])
    for u in us[1:]:
        lowest = jnp.minimum(lowest, jnp.min(cs_all[u]))
    in_range = lowest >= -GLA_SAFE_DECAY

    @pl.when(in_range)
    def _():
        tri, _, _ = _chunk_masks(reverse)
        last = 0 if reverse else CHUNK - 1
        ecs = [jnp.exp(cs_all[u]) for u in us]
        qe = [q[u] * ecs[u] for u in us]
        kd = [key[u] / ecs[u] for u in us]
        tot = [jnp.exp(cs_all[u][last:last + 1]) for u in us]
        st = [s_ref[u] for u in us]
        sc = [_dot_nt(qe[u], kd[u]) for u in us]
        o_in = [_dot_nt(qe[u], st[u]) for u in us]
        upd = [_dot_tn(v[:, vcol[u]], kd[u] * tot[u]) for u in us]
        o_di = [_dot(jnp.where(tri, sc[u], 0.0), v[:, vcol[u]]) for u in us]
        for u in us:
            s_ref[u] = st[u] * tot[u] + upd[u]
            emit(0, CHUNK, u, o_in[u] + o_di[u])

    @pl.when(jnp.logical_not(in_range))
    def _():
        logf_w = jnp.concatenate(logf, axis=1)
        key_w = jnp.concatenate(key, axis=1)
        q_w = jnp.concatenate(q, axis=1)
        cs = _scan_rows(logf_w, SUB, reverse)
        qe = q_w * jnp.exp(cs)
        nsub = CHUNK // SUB
        s16 = _iota((SUB, LANE), 0)
        t16 = _iota((SUB, LANE), 1)
        keep = (s16 >= t16) if reverse else (s16 <= t16)
        last = 0 if reverse else SUB - 1
        order = list(range(nsub - 1, -1, -1) if reverse else range(nsub))
        rows = [slice(i * SUB, (i + 1) * SUB) for i in range(nsub)]
        attn_t = []
        for i in range(nsub):
            cs_i, key_i, q_i = cs[rows[i]], key_w[rows[i]], q_w[rows[i]]
            acc = [jnp.zeros((SUB, LANE), F32) for _ in us]
            for t in range(SUB):
                p = (q_i[t:t + 1] * key_i) * jnp.exp(jnp.minimum(cs_i[t:t + 1] - cs_i, 0.0))
                for u in us:
                    acc[u] = jnp.where(t16 == t, jnp.sum(p[:, kcol[u]], axis=1, keepdims=True), acc[u])
            attn_t.append([jnp.where(keep, acc[u], 0.0) for u in us])
        tot = [cs[i * SUB + last:i * SUB + last + 1] for i in range(nsub)]
        kd = [key_w[rows[i]] * jnp.exp(tot[i] - cs[rows[i]]) for i in range(nsub)]
        st = [s_ref[u] for u in us]
        for i in order:
            o_in = [_dot_nt(qe[rows[i], kcol[u]], st[u]) for u in us]
            o_di = [_dot_tn(attn_t[i][u], v[rows[i], vcol[u]])[:SUB] for u in us]
            upd = [_dot_tn(v[rows[i], vcol[u]], kd[i][:, kcol[u]]) for u in us]
            st = [st[u] * jnp.exp(tot[i][:, kcol[u]]) + upd[u] for u in us]
            for u in us:
                emit(i * SUB, SUB, u, o_in[u] + o_di[u])
        for u in us:
            s_ref[u] = st[u]


def _gla(proj, lower_bounds, nc, heads, dk, dv, layer, of=None, norm=None):
    reverse = of is not None
    m = proj.shape[0]
    b = m // (nc * CHUNK)
    unit = _tile(heads, (8, 4, 2, 1))
    ng = heads // unit
    depth = lower_bounds.shape[0]
    assert dk == dv

    def row(i, c):
        return i * nc + (nc - 1 - c if reverse else c)

    in_specs = [
        pl.BlockSpec((CHUNK, unit * dk), lambda i, g, c: (row(i, c), g)),
        pl.BlockSpec((CHUNK, unit * dk), lambda i, g, c: (row(i, c), (2 if reverse else 1) * ng + g)),
        pl.BlockSpec((CHUNK, unit * dv), lambda i, g, c: (row(i, c), 3 * ng + g)),
        pl.BlockSpec((depth, unit * dk), lambda i, g, c: (0, g)),
    ]
    args = [proj, proj, proj, lower_bounds.astype(F32)]
    if reverse:
        in_specs += [
            pl.BlockSpec((CHUNK, unit * dv), lambda i, g, c: (row(i, c), g)),
            pl.BlockSpec((CHUNK, unit * dv), lambda i, g, c: (row(i, c), 4 * ng + g)),
            pl.BlockSpec((1, dv), lambda i, g, c: (0, 0)),
        ]
        args += [of, proj, norm.reshape(1, dv).astype(F32)]
    return pl.pallas_call(
        functools.partial(_gla_kernel, unit=unit, dk=dk, dv=dv, layer=layer, reverse=reverse),
        out_shape=jax.ShapeDtypeStruct((m, heads * dv), BF16 if reverse else F32),
        grid=(b, ng, nc),
        in_specs=in_specs,
        out_specs=pl.BlockSpec((CHUNK, unit * dv), lambda i, g, c: (row(i, c), g)),
        scratch_shapes=[pltpu.VMEM((unit, dv, dk), F32)],
        compiler_params=_cp(("parallel", "parallel", "arbitrary")),
        name="hgrn2_bwd" if reverse else "hgrn2_fwd",
    )(*args)


def _diff_attn_kernel(q_ref, k_ref, v_ref, qn_ref, kn_ref, lam_ref, sn_ref, o_ref, kn_s, v_s, *, hd, heads, lam_init):
    h = pl.program_id(1)
    qi = pl.program_id(2)
    tq = q_ref.shape[0]
    t = k_ref.shape[0]

    @pl.when(qi == 0)
    def _():
        kk = k_ref[...]
        for mp in range(2):
            km = kk[:, mp * hd:(mp + 1) * hd]
            ms = jnp.mean(km * km, axis=1, keepdims=True)
            kn_s[mp] = (km * lax.rsqrt(ms + EPS) * kn_ref[...]).astype(BF16)
        v_s[...] = v_ref[...].astype(BF16)

    lv = lam_ref[...]
    lam = (jnp.exp(jnp.sum(lv[0:1] * lv[1:2], axis=1, keepdims=True))
           - jnp.exp(jnp.sum(lv[2:3] * lv[3:4], axis=1, keepdims=True)) + lam_init)
    slope2 = jnp.exp2(-8.0 * (jnp.full((1, 1), h + 1, jnp.int32).astype(F32)) / heads) * LOG2E
    kpos = _iota((1, t), 1)
    kbias = jnp.where(kpos >= PAD, slope2 * kpos.astype(F32), NEG)
    rb = _tile(tq, (88, 64, 32, 16, 8))
    blocks = range(0, tq, rb)
    ridx = [qi * tq + r0 + _iota((rb, 1), 0) for r0 in blocks]
    nb = len(ridx)
    s = []
    for r0 in blocks:
        q = q_ref[r0:r0 + rb, :]
        for mp in range(2):
            qm = q[:, mp * hd:(mp + 1) * hd]
            ms = jnp.mean(qm * qm, axis=1, keepdims=True)
            qn = qm * lax.rsqrt(ms + EPS) * qn_ref[...]
            s.append(lax.dot_general(qn.astype(BF16), kn_s[mp], (((1,), (1,)), ((), ())),
                                     preferred_element_type=F32))
    a = []
    for i in range(nb):
        bias2 = jnp.abs(slope2 * ridx[i].astype(F32) - kbias)
        es, inv = [], []
        for mp in range(2):
            x = s[2 * i + mp] * (hd ** -0.5 * LOG2E) - bias2
            e = jnp.exp2(x - jnp.max(x, axis=1, keepdims=True))
            es.append(e)
            inv.append(1.0 / jnp.sum(e, axis=1, keepdims=True))
        a.append((es[0] * inv[0] - es[1] * (lam * inv[1])).astype(BF16))
    o = [jnp.dot(a[i], v_s[...], preferred_element_type=F32) for i in range(nb)]
    for i, r0 in enumerate(blocks):
        ms = jnp.mean(o[i] * o[i], axis=1, keepdims=True)
        y = o[i] * lax.rsqrt(ms + EPS) * sn_ref[...] * (1.0 - lam_init)
        o_ref[r0:r0 + rb, :] = jnp.where(ridx[i] >= PAD, y, 0.0).astype(o_ref.dtype)


def _diff_attn(proj, col_off, t, heads, hd, q_norm, k_norm, lam_vec, sub_norm, lam_init):
    m = proj.shape[0]
    b = m // t
    w = 2 * hd
    tq = _tile(t, (264, 192, 64))
    nq = t // tq
    c0 = col_off // w
    return pl.pallas_call(
        functools.partial(_diff_attn_kernel, hd=hd, heads=heads, lam_init=lam_init),
        out_shape=jax.ShapeDtypeStruct((m, heads * w), BF16),
        grid=(b, heads, nq),
        in_specs=[
            pl.BlockSpec((tq, w), lambda i, h, j: (i * nq + j, c0 + h)),
            pl.BlockSpec((t, w), lambda i, h, j: (i, c0 + heads + h)),
            pl.BlockSpec((t, w), lambda i, h, j: (i, c0 + 2 * heads + h)),
            pl.BlockSpec((1, hd), lambda i, h, j: (0, 0)),
            pl.BlockSpec((1, hd), lambda i, h, j: (0, 0)),
            pl.BlockSpec((4, hd), lambda i, h, j: (0, 0)),
            pl.BlockSpec((1, w), lambda i, h, j: (0, 0)),
        ],
        out_specs=pl.BlockSpec((tq, w), lambda i, h, j: (i * nq + j, h)),
        scratch_shapes=[pltpu.VMEM((2, t, hd), BF16), pltpu.VMEM((t, w), BF16)],
        compiler_params=_cp(("parallel", "parallel", "arbitrary")),
        name="diff_attn",
    )(proj, proj, proj, q_norm.reshape(1, hd).astype(F32), k_norm.reshape(1, hd).astype(F32),
      lam_vec.astype(F32), sub_norm.reshape(1, w).astype(F32))


def _place(width, pieces):
    row = jnp.zeros((width,), F32)
    for off, vals in pieces:
        row = lax.dynamic_update_slice(row, vals.astype(F32).reshape(-1), (off,))
    return row.reshape(1, width)


def _trunk(h, t, p, bp):
    m, d = h.shape
    nc = t // CHUNK
    mix_w = d // 2
    depth = p["norm_mix"].shape[0]
    ha = p["mlstm_i_bias"].shape[-1]
    a_dv = mix_w // ha
    a_dqk = a_dv // 2
    hb = p["dn_a_log"].shape[-1]
    b_dk = mix_w // hb
    c_dv = p["hgrn_norm"].shape[-1]
    hc = mix_w // c_dv
    c_dk = p["hgrn_lower_bounds"].shape[-1] // hc
    d_hd = p["diff_q_norm"].shape[-1]
    hdh = mix_w // (2 * d_hd)
    o_ag = 2 * ha * a_dqk + 2 * ha * a_dv
    o_qkv = o_ag + 4 * ha
    o_bz = o_qkv + 3 * mix_w
    o_bg = o_bz + mix_w
    assert 4 * ha + 4 * hb <= LANE
    w_up = p["ffn_w_up"].astype(BF16)
    w_down = p["ffn_w_down"].astype(BF16)
    for l in range(depth):
        hn = _rmsnorm_bf16(h, p["norm_mix"][l])
        if l % 2 == 0:
            e = l // 2
            w_in = p["ab_w_in"][e]
            w_plain = jnp.concatenate([w_in[:, :o_ag], w_in[:, o_bz:o_bg]], axis=1).astype(BF16)
            w_gate = jnp.concatenate(
                [w_in[:, o_ag:o_qkv], w_in[:, o_bg:], jnp.zeros((d, LANE - 4 * ha - 4 * hb), F32)], axis=1).astype(BF16)
            w_qkv = w_in[:, o_qkv:o_bz].astype(BF16)
            proj = _matmul(hn, w_plain, name="ab_proj")
            gates = _matmul(hn, w_gate, name="ab_gates")
            qkv = _qkv_conv(hn, w_qkv, p["dn_conv"][e], t, b_dk)
            bias_row = _place(LANE, [(dd * 2 * ha + gg * ha, (p["mlstm_i_bias"][e], p["mlstm_f_bias"][e])[gg][dd])
                                     for dd in range(2) for gg in range(2)])
            yf = _mlstm(proj, gates, bias_row, nc, ha, a_dqk, a_dv)
            xa = _mlstm(proj, gates, bias_row, nc, ha, a_dqk, a_dv, yf=yf, norm=p["mlstm_norm"][e])
            dt_row = _place(LANE, [(4 * ha + dd * 2 * hb, p["dn_dt_bias"][e][dd]) for dd in range(2)])
            alog_row = _place(LANE, [(4 * ha + dd * 2 * hb, p["dn_a_log"][e][dd]) for dd in range(2)])
            of = _gdn(qkv, gates, dt_row, alog_row, nc, hb, b_dk, b_dk, 4 * ha)
            xb = _gdn(qkv, gates, dt_row, alog_row, nc, hb, b_dk, b_dk, 4 * ha, of=of, z=proj, z_off=o_ag,
                      norm=p["dn_norm"][e])
            w_out = p["ab_w_out"][e].astype(BF16)
        else:
            o = l // 2
            proj = _matmul(hn, p["cd_w_in"][o].astype(BF16), name="cd_proj")
            of = _gla(proj, p["hgrn_lower_bounds"], nc, hc, c_dk, c_dv, l)
            xa = _gla(proj, p["hgrn_lower_bounds"], nc, hc, c_dk, c_dv, l, of=of, norm=p["hgrn_norm"][o])
            lam_init = 0.8 - 0.6 * math.exp(-0.3 * l)
            xb = _diff_attn(proj, 3 * hc * c_dk + 2 * hc * c_dv, t, hdh, d_hd, p["diff_q_norm"][o],
                            p["diff_k_norm"][o], p["diff_lambda"][o], p["diff_norm"][o], lam_init)
            w_out = p["cd_w_out"][o].astype(BF16)
        h = _matmul2_residual(xa, xb, w_out, h)
        hn = _rmsnorm_bf16(h, p["norm_ffn"][l])
        act = _ffn_up(hn, w_up, l, p["ffn_conv_w"][l], p["ffn_conv_b"][l], t)
        if l + 1 < depth:
            h = _matmul_residual(act, w_down, l, h)
    b = m // t
    s = t - PAD - N_META
    y_p = _matmul_residual_tokens(act, w_down, depth - 1, h, t, 0, bp, name="ffn_down_prompt")
    y_s = _matmul_residual_tokens(act, w_down, depth - 1, h, t, bp, b - bp, name="ffn_down_sample")
    return y_p.reshape(bp, s, d), y_s.reshape(b - bp, s, d)


def kernel(x_prompt, x_sample, meta_tokens, norm_mix, norm_ffn, ab_w_in, ab_w_out, mlstm_i_bias, mlstm_f_bias, mlstm_norm, dn_conv, dn_a_log, dn_dt_bias, dn_norm, cd_w_in, cd_w_out, hgrn_lower_bounds, hgrn_norm, diff_q_norm, diff_k_norm, diff_lambda, diff_norm, ffn_w_up, ffn_conv_w, ffn_conv_b, ffn_w_down):
    p = dict(norm_mix=norm_mix, norm_ffn=norm_ffn, ab_w_in=ab_w_in, ab_w_out=ab_w_out, mlstm_i_bias=mlstm_i_bias,
             mlstm_f_bias=mlstm_f_bias, mlstm_norm=mlstm_norm, dn_conv=dn_conv, dn_a_log=dn_a_log,
             dn_dt_bias=dn_dt_bias, dn_norm=dn_norm, cd_w_in=cd_w_in, cd_w_out=cd_w_out,
             hgrn_lower_bounds=hgrn_lower_bounds, hgrn_norm=hgrn_norm, diff_q_norm=diff_q_norm,
             diff_k_norm=diff_k_norm, diff_lambda=diff_lambda, diff_norm=diff_norm, ffn_w_up=ffn_w_up,
             ffn_conv_w=ffn_conv_w, ffn_conv_b=ffn_conv_b, ffn_w_down=ffn_w_down)
    bp, s, d = x_prompt.shape
    bs = x_sample.shape[0]
    assert x_sample.shape[1:] == (s, d)
    t = PAD + N_META + s
    assert t % CHUNK == 0
    b = bp + bs
    front = jnp.concatenate([jnp.zeros((PAD, d), F32), meta_tokens.astype(F32)], axis=0)
    h = jnp.concatenate(
        [jnp.concatenate([jnp.broadcast_to(front[None], (n, PAD + N_META, d)), x.astype(F32)], axis=1)
         for n, x in ((bp, x_prompt), (bs, x_sample))], axis=0).reshape(b * t, d)
    return _trunk(h, t, p, bp)
```

```python
import functools
import math

import jax
import jax.numpy as jnp
from jax import lax
from jax.experimental import pallas as pl
from jax.experimental.pallas import tpu as pltpu

F32 = jnp.float32
BF16 = jnp.bfloat16
EPS = 1e-6
NEG = -1e30
N_META = 16
CHUNK = 64
PAD = CHUNK - N_META
SUB = 16
GLA_SAFE_DECAY = 60.0
LANE = 128
LOG2E = 1.4426950408889634
VMEM_LIMIT = 56 * 1024 * 1024


def _cp(sem, vmem=VMEM_LIMIT):
    return pltpu.CompilerParams(dimension_semantics=sem, vmem_limit_bytes=vmem)


def _iota(shape, dim):
    return lax.broadcasted_iota(jnp.int32, shape, dim)


def _dot(a, b):
    return jnp.dot(a.astype(BF16), b.astype(BF16), preferred_element_type=F32)


def _dot_nt(a, b):
    return lax.dot_general(a.astype(BF16), b.astype(BF16), (((1,), (1,)), ((), ())), preferred_element_type=F32)


def _dot_tn(a, b):
    return lax.dot_general(a.astype(BF16), b.astype(BF16), (((0,), (0,)), ((), ())), preferred_element_type=F32)


def _log_sigmoid(x):
    return jnp.minimum(x, 0.0) - jnp.log1p(jnp.exp(-jnp.abs(x)))


def _softplus(x):
    return jnp.maximum(x, 0.0) + jnp.log1p(jnp.exp(-jnp.abs(x)))


def _sigmoid(x):
    return 1.0 / (1.0 + jnp.exp(-x))


def _silu(x):
    return x * _sigmoid(x)


def _tile(n, cands):
    for c in cands:
        if n % c == 0:
            return c
    return n


def _rms_kernel(x_ref, g_ref, o_ref):
    x = x_ref[...]
    ms = jnp.mean(x * x, axis=-1, keepdims=True)
    o_ref[...] = (x * lax.rsqrt(ms + EPS) * g_ref[...]).astype(o_ref.dtype)


def _rmsnorm_bf16(x, g):
    m, d = x.shape
    tm = _tile(m, (528, 264, 192, 64))
    return pl.pallas_call(
        _rms_kernel,
        out_shape=jax.ShapeDtypeStruct((m, d), BF16),
        grid=(m // tm,),
        in_specs=[pl.BlockSpec((tm, d), lambda i: (i, 0)), pl.BlockSpec((1, d), lambda i: (0, 0))],
        out_specs=pl.BlockSpec((tm, d), lambda i: (i, 0)),
        compiler_params=_cp(("parallel",)),
        name="rmsnorm",
    )(x, g.reshape(1, d).astype(F32))


def _mm_kernel(x_ref, w_ref, o_ref):
    o_ref[...] = jnp.dot(x_ref[...], w_ref[...], preferred_element_type=F32).astype(o_ref.dtype)


def _matmul(x, w, out_dtype=F32, name="matmul"):
    m, k = x.shape
    n = w.shape[1]
    tm = _tile(m, (1056, 704, 192, 64))
    tn = _tile(n, (512, 256, 128))
    return pl.pallas_call(
        _mm_kernel,
        out_shape=jax.ShapeDtypeStruct((m, n), out_dtype),
        grid=(m // tm, n // tn),
        in_specs=[pl.BlockSpec((tm, k), lambda i, j: (i, 0)), pl.BlockSpec((k, tn), lambda i, j: (0, j))],
        out_specs=pl.BlockSpec((tm, tn), lambda i, j: (i, j)),
        compiler_params=_cp(("parallel", "arbitrary")),
        name=name,
    )(x, w)


def _mm2_res_kernel(xa_ref, xb_ref, wa_ref, wb_ref, r_ref, o_ref):
    acc = jnp.dot(xa_ref[...], wa_ref[...], preferred_element_type=F32)
    acc = acc + jnp.dot(xb_ref[...], wb_ref[...], preferred_element_type=F32)
    o_ref[...] = r_ref[...] + acc


def _matmul2_residual(xa, xb, w, res, name="mix_out"):
    m, ka = xa.shape
    kb = xb.shape[1]
    n = w.shape[1]
    tm = _tile(m, (1056, 704, 192, 64))
    tn = _tile(n, (512, 256, 128))
    assert ka == kb
    return pl.pallas_call(
        _mm2_res_kernel,
        out_shape=jax.ShapeDtypeStruct((m, n), F32),
        grid=(m // tm, n // tn),
        in_specs=[
            pl.BlockSpec((tm, ka), lambda i, j: (i, 0)),
            pl.BlockSpec((tm, kb), lambda i, j: (i, 0)),
            pl.BlockSpec((ka, tn), lambda i, j: (0, j)),
            pl.BlockSpec((kb, tn), lambda i, j: (1, j)),
            pl.BlockSpec((tm, tn), lambda i, j: (i, j)),
        ],
        out_specs=pl.BlockSpec((tm, tn), lambda i, j: (i, j)),
        compiler_params=_cp(("parallel", "arbitrary")),
        name=name,
    )(xa, xb, w, w, res)


def _mm_res_kernel(x_ref, w_ref, r_ref, o_ref):
    o_ref[...] = r_ref[...] + jnp.dot(x_ref[...], w_ref[...], preferred_element_type=F32)


def _matmul_residual(x, w, layer, res, name="ffn_down"):
    m, k = x.shape
    n = w.shape[2]
    tm = _tile(m, (704, 192, 64))
    tn = _tile(n, (256, 128))
    return pl.pallas_call(
        _mm_res_kernel,
        out_shape=jax.ShapeDtypeStruct((m, n), F32),
        grid=(m // tm, n // tn),
        in_specs=[
            pl.BlockSpec((tm, k), lambda i, j: (i, 0)),
            pl.BlockSpec((None, k, tn), lambda i, j: (layer, 0, j)),
            pl.BlockSpec((tm, tn), lambda i, j: (i, j)),
        ],
        out_specs=pl.BlockSpec((tm, tn), lambda i, j: (i, j)),
        compiler_params=_cp(("parallel", "arbitrary")),
        name=name,
    )(x, w, res)


def _matmul_residual_tokens(x, w, layer, res, t, seq0, nseq, name="ffn_down_out"):
    m, k = x.shape
    n = w.shape[2]
    s = t - PAD - N_META
    tm = _tile(s, (512, 128, 64))
    per = s // tm
    tn = _tile(n, (256, 128))

    def row0(a, r):
        return pl.multiple_of((seq0 + a) * t + (PAD + N_META) + r * tm, CHUNK)

    return pl.pallas_call(
        _mm_res_kernel,
        out_shape=jax.ShapeDtypeStruct((nseq * s, n), F32),
        grid=(nseq, per, n // tn),
        in_specs=[
            pl.BlockSpec((pl.Element(tm), pl.Element(k)), lambda a, r, j: (row0(a, r), 0)),
            pl.BlockSpec((None, k, tn), lambda a, r, j: (layer, 0, j)),
            pl.BlockSpec((pl.Element(tm), pl.Element(tn)), lambda a, r, j: (row0(a, r), j * tn)),
        ],
        out_specs=pl.BlockSpec((tm, tn), lambda a, r, j: (a * per + r, j)),
        compiler_params=_cp(("parallel", "parallel", "arbitrary")),
        name=name,
    )(x, w, res)


def _conv3_rows(y, cw):
    t = y.shape[0]
    return cw[0:1] * pltpu.roll(y, 1, 0) + cw[1:2] * y + cw[2:3] * pltpu.roll(y, t - 1, 0)


def _ffn_up_kernel(x_ref, wg_ref, wv_ref, cw_ref, cb_ref, o_ref):
    x = x_ref[...]
    g = jnp.dot(x, wg_ref[...], preferred_element_type=F32)
    v = jnp.dot(x, wv_ref[...], preferred_element_type=F32)
    gc = _conv3_rows(g, cw_ref[...]) + cb_ref[...]
    o_ref[...] = (_silu(gc) * v).astype(o_ref.dtype)


def _ffn_up(xn, w_up, layer, conv_w, conv_b, t):
    m, d = xn.shape
    f = conv_w.shape[1]
    tn = _tile(f, (256, 128))
    nj = f // tn
    return pl.pallas_call(
        _ffn_up_kernel,
        out_shape=jax.ShapeDtypeStruct((m, f), BF16),
        grid=(m // t, nj),
        in_specs=[
            pl.BlockSpec((t, d), lambda i, j: (i, 0), pipeline_mode=pl.Buffered(1)),
            pl.BlockSpec((None, d, tn), lambda i, j: (layer, 0, j)),
            pl.BlockSpec((None, d, tn), lambda i, j: (layer, 0, j + nj)),
            pl.BlockSpec((3, tn), lambda i, j: (0, j)),
            pl.BlockSpec((1, tn), lambda i, j: (0, j)),
        ],
        out_specs=pl.BlockSpec((t, tn), lambda i, j: (i, j)),
        compiler_params=_cp(("parallel", "arbitrary")),
        name="ffn_up",
    )(xn, w_up, w_up, conv_w.astype(F32), conv_b.reshape(1, f).astype(F32))


def _qkv_conv_kernel(x_ref, w_ref, cw_ref, o_ref, *, nq, qscale):
    j = pl.program_id(1)
    y = jnp.dot(x_ref[...], w_ref[...], preferred_element_type=F32)
    a = _silu(_conv3_rows(y, cw_ref[...]))
    t, tn = a.shape
    a = jnp.where(_iota((t, 1), 0) >= PAD, a, 0.0)
    is_qk = j < 2 * nq
    scale = jnp.where(j < nq, qscale, 1.0)
    for s in range(tn // LANE):
        seg = a[:, s * LANE:(s + 1) * LANE]
        ss = jnp.sum(seg * seg, axis=1, keepdims=True)
        nrm = seg * lax.rsqrt(ss + EPS) * scale
        o_ref[:, s * LANE:(s + 1) * LANE] = jnp.where(is_qk, nrm, seg)


def _qkv_conv(xn, w, conv_w, t, dk):
    m, d = xn.shape
    n = w.shape[1]
    tn = _tile(n // 3, (256, 128))
    return pl.pallas_call(
        functools.partial(_qkv_conv_kernel, nq=(n // 3) // tn, qscale=dk ** -0.5),
        out_shape=jax.ShapeDtypeStruct((m, n), F32),
        grid=(m // t, n // tn),
        in_specs=[
            pl.BlockSpec((t, d), lambda i, j: (i, 0), pipeline_mode=pl.Buffered(1)),
            pl.BlockSpec((d, tn), lambda i, j: (0, j)),
            pl.BlockSpec((3, tn), lambda i, j: (0, j)),
        ],
        out_specs=pl.BlockSpec((t, tn), lambda i, j: (i, j)),
        compiler_params=_cp(("parallel", "arbitrary")),
        name="dn_qkv_conv",
    )(xn, w, conv_w.astype(F32))


def _chunk_masks(reverse):
    r = _iota((CHUNK, CHUNK), 0)
    s = _iota((CHUNK, CHUNK), 1)
    tri = (s >= r) if reverse else (s <= r)
    strict = (s > r) if reverse else (s < r)
    return tri, strict, r == s


def _scan_rows(x, seg, reverse):
    rows = x.shape[0]
    r = _iota((rows, 1), 0) % seg
    s = 1
    while s < seg:
        if reverse:
            x = x + jnp.where(r < seg - s, pltpu.roll(x, rows - s, 0), 0.0)
        else:
            x = x + jnp.where(r >= s, pltpu.roll(x, s, 0), 0.0)
        s *= 2
    return x


def _transpose_tile(x):
    return jnp.concatenate([x, jnp.zeros((LANE - x.shape[0], LANE), F32)], axis=0).T


def _valid_rows(chunk):
    return (chunk * CHUNK + _iota((CHUNK, 1), 0)) >= PAD


def _head_norm_gate(y, gain, gate, valid):
    ms = jnp.mean(y * y, axis=1, keepdims=True)
    return jnp.where(valid, y * lax.rsqrt(ms + EPS) * gain * gate, 0.0)


def _mlstm_kernel(*refs, nseq, heads, dk, dv, reverse):
    if reverse:
        q_ref, k_ref, v_ref, g_ref, bias_ref, yf_ref, ao_ref, nrm_ref, o_ref, c_ref, n_ref, m_ref = refs
    else:
        q_ref, k_ref, v_ref, g_ref, bias_ref, o_ref, c_ref, n_ref, m_ref = refs
    step = pl.program_id(1)
    chunk = pl.num_programs(1) - 1 - step if reverse else step

    @pl.when(step == 0)
    def _():
        c_ref[...] = jnp.zeros_like(c_ref)
        n_ref[...] = jnp.zeros_like(n_ref)
        m_ref[...] = jnp.full_like(m_ref, NEG)

    valid = _valid_rows(chunk)
    tri, _, _ = _chunk_masks(reverse)
    last = 0 if reverse else CHUNK - 1
    base = 2 * heads if reverse else 0
    i_all, cum, i_t, cum_t = [], [], [], []
    for sq in range(nseq):
        gt = g_ref[sq] + bias_ref[...]
        i_all.append(jnp.where(valid, gt, NEG))
        cum.append(_scan_rows(jnp.where(valid, _log_sigmoid(gt), 0.0), CHUNK, reverse))
        i_t.append(_transpose_tile(i_all[sq]))
        cum_t.append(_transpose_tile(cum[sq]))
    units = [(sq, h) for sq in range(nseq) for h in range(heads)]
    us = range(len(units))
    ci = [base + h for _, h in units]
    cf = [base + heads + h for _, h in units]
    i_col = [i_all[sq][:, ci[u]:ci[u] + 1] for u, (sq, _) in enumerate(units)]
    i_row = [i_t[sq][ci[u]:ci[u] + 1, :CHUNK] for u, (sq, _) in enumerate(units)]
    b_col = [cum[sq][:, cf[u]:cf[u] + 1] for u, (sq, _) in enumerate(units)]
    b_row = [cum_t[sq][cf[u]:cf[u] + 1, :CHUNK] for u, (sq, _) in enumerate(units)]
    b_last = [cum[sq][last:last + 1, cf[u]:cf[u] + 1] for u, (sq, _) in enumerate(units)]
    q = [q_ref[sq, :, h * dk:(h + 1) * dk] for sq, h in units]
    k = [k_ref[sq, :, h * dk:(h + 1) * dk] * (dk ** -0.5) for sq, h in units]
    v = [v_ref[sq, :, h * dv:(h + 1) * dv] for sq, h in units]
    ct = [c_ref[u] for u in us]
    n_row = [n_ref[u] for u in us]
    m = [m_ref[u][:, 0:1] for u in us]
    qk = [_dot_nt(q[u], k[u]) for u in us]
    qc = [_dot(q[u], ct[u]) for u in us]
    dmat = [jnp.where(tri, b_col[u] - b_row[u] + i_row[u], NEG) for u in us]
    inter = [b_col[u] + m[u] for u in us]
    m_t = [jnp.maximum(inter[u], jnp.max(dmat[u], axis=1, keepdims=True)) for u in us]
    w = [jnp.exp(dmat[u] - m_t[u]) * qk[u] for u in us]
    s_inter = [jnp.exp(inter[u] - m_t[u]) for u in us]
    wv = [_dot(w[u], v[u]) for u in us]
    dec = [b_last[u] - b_col[u] + i_col[u] for u in us]
    m_new = [jnp.maximum(b_last[u] + m[u], jnp.max(dec[u], axis=0, keepdims=True)) for u in us]
    a_old = [jnp.exp(b_last[u] + m[u] - m_new[u]) for u in us]
    w_in = [jnp.exp(dec[u] - m_new[u]) for u in us]
    kv = [_dot_tn(k[u], w_in[u] * v[u]) for u in us]
    for u, (sq, h) in enumerate(units):
        num = s_inter[u] * qc[u] + wv[u]
        den = (s_inter[u] * jnp.sum(q[u] * n_row[u], axis=1, keepdims=True)
               + jnp.sum(w[u], axis=1, keepdims=True))
        hout = num / jnp.maximum(jnp.abs(den), jnp.exp(-m_t[u]))
        c_ref[u] = a_old[u] * ct[u] + kv[u]
        n_ref[u] = a_old[u] * n_row[u] + jnp.sum(w_in[u] * k[u], axis=0, keepdims=True)
        m_ref[u] = jnp.broadcast_to(m_new[u], (1, LANE))
        cols = slice(h * dv, (h + 1) * dv)
        if reverse:
            y = yf_ref[sq, :, cols] + hout
            o_ref[sq, :, cols] = _head_norm_gate(
                y, nrm_ref[:, cols], _sigmoid(ao_ref[sq, :, cols]), valid).astype(o_ref.dtype)
        else:
            o_ref[sq, :, cols] = hout


def _mlstm(proj, gates, bias_row, nc, heads, dk, dv, yf=None, norm=None):
    reverse = yf is not None
    m = proj.shape[0]
    t = nc * CHUNK
    b = m // t
    wq, wv = heads * dk, heads * dv
    nseq = _tile(b, (4, 3, 2, 1))
    proj3 = proj.reshape(b, t, proj.shape[1])
    gates3 = gates.reshape(b, t, LANE)

    def ch(c):
        return nc - 1 - c if reverse else c

    in_specs = [
        pl.BlockSpec((nseq, CHUNK, wq), lambda i, c: (i, ch(c), 0)),
        pl.BlockSpec((nseq, CHUNK, wq), lambda i, c: (i, ch(c), 1)),
        pl.BlockSpec((nseq, CHUNK, wv), lambda i, c: (i, ch(c), (2 * wq) // wv)),
        pl.BlockSpec((nseq, CHUNK, LANE), lambda i, c: (i, ch(c), 0)),
        pl.BlockSpec((1, LANE), lambda i, c: (0, 0)),
    ]
    args = [proj3, proj3, proj3, gates3, bias_row]
    if reverse:
        in_specs += [
            pl.BlockSpec((nseq, CHUNK, wv), lambda i, c: (i, ch(c), 0)),
            pl.BlockSpec((nseq, CHUNK, wv), lambda i, c: (i, ch(c), (2 * wq) // wv + 1)),
            pl.BlockSpec((1, wv), lambda i, c: (0, 0)),
        ]
        args += [yf.reshape(b, t, wv), proj3, norm.reshape(1, wv).astype(F32)]
    out = pl.pallas_call(
        functools.partial(_mlstm_kernel, nseq=nseq, heads=heads, dk=dk, dv=dv, reverse=reverse),
        out_shape=jax.ShapeDtypeStruct((b, t, wv), BF16 if reverse else F32),
        grid=(b // nseq, nc),
        in_specs=in_specs,
        out_specs=pl.BlockSpec((nseq, CHUNK, wv), lambda i, c: (i, ch(c), 0)),
        scratch_shapes=[
            pltpu.VMEM((nseq * heads, dk, dv), F32),
            pltpu.VMEM((nseq * heads, 1, dk), F32),
            pltpu.VMEM((nseq * heads, 1, LANE), F32),
        ],
        compiler_params=_cp(("parallel", "arbitrary")),
        name="mlstm_bwd" if reverse else "mlstm_fwd",
    )(*args)
    return out.reshape(m, wv)


def _gdn_kernel(*refs, nseq, heads, dk, dv, gate_off, reverse):
    if reverse:
        q_ref, k_ref, v_ref, g_ref, dt_ref, alog_ref, of_ref, z_ref, nrm_ref, o_ref, s_ref = refs
    else:
        q_ref, k_ref, v_ref, g_ref, dt_ref, alog_ref, o_ref, s_ref = refs
    step = pl.program_id(1)
    chunk = pl.num_programs(1) - 1 - step if reverse else step

    @pl.when(step == 0)
    def _():
        s_ref[...] = jnp.zeros_like(s_ref)

    valid = _valid_rows(chunk)
    tri, strict, eye = _chunk_masks(reverse)
    eye_f = jnp.where(eye, 1.0, 0.0)
    last = 0 if reverse else CHUNK - 1
    base = gate_off + (2 * heads if reverse else 0)
    cum, cum_t, beta_all = [], [], []
    for sq in range(nseq):
        raw = g_ref[sq]
        decay = jnp.where(valid, -jnp.exp(alog_ref[...]) * _softplus(raw + dt_ref[...]), 0.0)
        beta_all.append(jnp.where(valid, _sigmoid(raw), 0.0))
        cum.append(_scan_rows(decay, CHUNK, reverse))
        cum_t.append(_transpose_tile(cum[sq]))
    units = [(sq, h) for sq in range(nseq) for h in range(heads)]
    us = range(len(units))
    b_col = [cum[sq][:, base + h:base + h + 1] for sq, h in units]
    b_row = [cum_t[sq][base + h:base + h + 1, :CHUNK] for sq, h in units]
    b_last = [cum[sq][last:last + 1, base + h:base + h + 1] for sq, h in units]
    beta = [beta_all[sq][:, base + heads + h:base + heads + h + 1] for sq, h in units]
    q = [q_ref[sq, :, h * dk:(h + 1) * dk] for sq, h in units]
    k = [k_ref[sq, :, h * dk:(h + 1) * dk] for sq, h in units]
    v = [v_ref[sq, :, h * dv:(h + 1) * dv] for sq, h in units]
    s = [s_ref[u] for u in us]
    kb = [k[u] * beta[u] for u in us]
    kk = [_dot_nt(kb[u], k[u]) for u in us]
    qk = [_dot_nt(q[u], k[u]) for u in us]
    decay = [jnp.exp(jnp.where(tri, b_col[u] - b_row[u], NEG)) for u in us]
    eb = [jnp.exp(b_col[u]) for u in us]
    qs = [_dot(q[u] * eb[u], s[u]) for u in us]
    x = [jnp.where(strict, kk[u] * decay[u], 0.0) for u in us]
    rhs = [jnp.concatenate([v[u] * beta[u], kb[u] * eb[u]], axis=1) for u in us]
    p = [eye_f - x[u] for u in us]
    n = 1
    while 2 * n < CHUNK:
        x = [_dot(x[u], x[u]) for u in us]
        p = [p[u] + _dot(p[u], x[u]) for u in us]
        n *= 2
    sol = [_dot(p[u], rhs[u]) for u in us]
    ws = [_dot(sol[u][:, dv:], s[u]) for u in us]
    v_new = [sol[u][:, :dv] - ws[u] for u in us]
    attn = [qk[u] * decay[u] for u in us]
    av = [_dot(attn[u], v_new[u]) for u in us]
    kv = [_dot_tn(k[u] * jnp.exp(b_last[u] - b_col[u]), v_new[u]) for u in us]
    for u, (sq, h) in enumerate(units):
        o = qs[u] + av[u]
        s_ref[u] = s[u] * jnp.exp(b_last[u]) + kv[u]
        cols = slice(h * dv, (h + 1) * dv)
        if reverse:
            y = of_ref[sq, :, cols] + o
            o_ref[sq, :, cols] = _head_norm_gate(
                y, nrm_ref[...], _silu(z_ref[sq, :, cols]), valid).astype(o_ref.dtype)
        else:
            o_ref[sq, :, cols] = o


def _gdn(qkv, gates, dt_row, alog_row, nc, heads, dk, dv, gate_off, of=None, z=None, z_off=0, norm=None):
    reverse = of is not None
    m = qkv.shape[0]
    t = nc * CHUNK
    b = m // t
    wk, wv = heads * dk, heads * dv
    nseq = _tile(b, (2, 1))
    qkv3 = qkv.reshape(b, t, qkv.shape[1])

    def ch(c):
        return nc - 1 - c if reverse else c

    in_specs = [
        pl.BlockSpec((nseq, CHUNK, wk), lambda i, c: (i, ch(c), 0)),
        pl.BlockSpec((nseq, CHUNK, wk), lambda i, c: (i, ch(c), 1)),
        pl.BlockSpec((nseq, CHUNK, wv), lambda i, c: (i, ch(c), (2 * wk) // wv)),
        pl.BlockSpec((nseq, CHUNK, LANE), lambda i, c: (i, ch(c), 0)),
        pl.BlockSpec((1, LANE), lambda i, c: (0, 0)),
        pl.BlockSpec((1, LANE), lambda i, c: (0, 0)),
    ]
    args = [qkv3, qkv3, qkv3, gates.reshape(b, t, LANE), dt_row, alog_row]
    if reverse:
        in_specs += [
            pl.BlockSpec((nseq, CHUNK, wv), lambda i, c: (i, ch(c), 0)),
            pl.BlockSpec((nseq, CHUNK, wv), lambda i, c: (i, ch(c), z_off // wv)),
            pl.BlockSpec((1, dv), lambda i, c: (0, 0)),
        ]
        args += [of.reshape(b, t, wv), z.reshape(b, t, z.shape[1]), norm.reshape(1, dv).astype(F32)]
    out = pl.pallas_call(
        functools.partial(_gdn_kernel, nseq=nseq, heads=heads, dk=dk, dv=dv, gate_off=gate_off, reverse=reverse),
        out_shape=jax.ShapeDtypeStruct((b, t, wv), BF16 if reverse else F32),
        grid=(b // nseq, nc),
        in_specs=in_specs,
        out_specs=pl.BlockSpec((nseq, CHUNK, wv), lambda i, c: (i, ch(c), 0)),
        scratch_shapes=[pltpu.VMEM((nseq * heads, dk, dv), F32)],
        compiler_params=_cp(("parallel", "arbitrary")),
        name="gdn_bwd" if reverse else "gdn_fwd",
    )(*args)
    return out.reshape(m, wv)


def _gla_kernel(*refs, unit, dk, dv, layer, reverse):
    if reverse:
        q_ref, f_ref, v_ref, lb_ref, of_ref, z_ref, nrm_ref, o_ref, s_ref = refs
    else:
        q_ref, f_ref, v_ref, lb_ref, o_ref, s_ref = refs
    step = pl.program_id(2)
    chunk = pl.num_programs(2) - 1 - step if reverse else step

    @pl.when(step == 0)
    def _():
        s_ref[...] = jnp.zeros_like(s_ref)

    valid = _valid_rows(chunk)
    lbm = lb_ref[...]
    e = jnp.exp(lbm - jnp.max(lbm, axis=0, keepdims=True))
    lb = jnp.sum(e[1:layer + 1], axis=0, keepdims=True) / jnp.sum(e, axis=0, keepdims=True)
    us = range(unit)
    kcol = [slice(u * dk, (u + 1) * dk) for u in us]
    vcol = [slice(u * dv, (u + 1) * dv) for u in us]
    logf, key, q, cs_all = [], [], [], []
    for u in us:
        x = f_ref[:, kcol[u]]
        lbu = lb[:, kcol[u]]
        z = jnp.exp(-jnp.abs(x))
        r = 1.0 / (1.0 + z)
        f = lbu + (1.0 - lbu) * (jnp.where(x >= 0.0, 1.0, z) * r)
        key.append(jnp.where(valid, (1.0 - lbu) * (jnp.where(x >= 0.0, z, 1.0) * r), 0.0))
        logf.append(jnp.where(valid, jnp.where(f > 0.0, jnp.log(f), x), 0.0))
        q.append(_silu(q_ref[:, kcol[u]]))
        cs_all.append(_scan_rows(logf[u], CHUNK, reverse))
    v = v_ref[...]

    def emit(r0, nrows, u, o):
        rs = slice(r0, r0 + nrows)
        if reverse:
            y = of_ref[rs, vcol[u]] + o
            ok = (chunk * CHUNK + r0 + _iota((nrows, 1), 0)) >= PAD
            o_ref[rs, vcol[u]] = _head_norm_gate(y, nrm_ref[...], _silu(z_ref[rs, vcol[u]]), ok).astype(o_ref.dtype)
        else:
            o_ref[rs, vcol[u]] = o

    lowest = jnp.min(cs_all[0])
    for u in us[1:]:
        lowest = jnp.minimum(lowest, jnp.min(cs_all[u]))
    in_range = lowest >= -GLA_SAFE_DECAY

    @pl.when(in_range)
    def _():
        tri, _, _ = _chunk_masks(reverse)
        last = 0 if reverse else CHUNK - 1
        ecs = [jnp.exp(cs_all[u]) for u in us]
        qe = [q[u] * ecs[u] for u in us]
        kd = [key[u] / ecs[u] for u in us]
        tot = [jnp.exp(cs_all[u][last:last + 1]) for u in us]
        st = [s_ref[u] for u in us]
        sc = [_dot_nt(qe[u], kd[u]) for u in us]
        o_in = [_dot_nt(qe[u], st[u]) for u in us]
        upd = [_dot_tn(v[:, vcol[u]], kd[u] * tot[u]) for u in us]
        o_di = [_dot(jnp.where(tri, sc[u], 0.0), v[:, vcol[u]]) for u in us]
        for u in us:
            s_ref[u] = st[u] * tot[u] + upd[u]
            emit(0, CHUNK, u, o_in[u] + o_di[u])

    @pl.when(jnp.logical_not(in_range))
    def _():
        logf_w = jnp.concatenate(logf, axis=1)
        key_w = jnp.concatenate(key, axis=1)
        q_w = jnp.concatenate(q, axis=1)
        cs = _scan_rows(logf_w, SUB, reverse)
        qe = q_w * jnp.exp(cs)
        nsub = CHUNK // SUB
        s16 = _iota((SUB, LANE), 0)
        t16 = _iota((SUB, LANE), 1)
        keep = (s16 >= t16) if reverse else (s16 <= t16)
        last = 0 if reverse else SUB - 1
        order = list(range(nsub - 1, -1, -1) if reverse else range(nsub))
        rows = [slice(i * SUB, (i + 1) * SUB) for i in range(nsub)]
        attn_t = []
        for i in range(nsub):
            cs_i, key_i, q_i = cs[rows[i]], key_w[rows[i]], q_w[rows[i]]
            acc = [jnp.zeros((SUB, LANE), F32) for _ in us]
            for t in range(SUB):
                p = (q_i[t:t + 1] * key_i) * jnp.exp(jnp.minimum(cs_i[t:t + 1] - cs_i, 0.0))
                for u in us:
                    acc[u] = jnp.where(t16 == t, jnp.sum(p[:, kcol[u]], axis=1, keepdims=True), acc[u])
            attn_t.append([jnp.where(keep, acc[u], 0.0) for u in us])
        tot = [cs[i * SUB + last:i * SUB + last + 1] for i in range(nsub)]
        kd = [key_w[rows[i]] * jnp.exp(tot[i] - cs[rows[i]]) for i in range(nsub)]
        st = [s_ref[u] for u in us]
        for i in order:
            o_in = [_dot_nt(qe[rows[i], kcol[u]], st[u]) for u in us]
            o_di = [_dot_tn(attn_t[i][u], v[rows[i], vcol[u]])[:SUB] for u in us]
            upd = [_dot_tn(v[rows[i], vcol[u]], kd[i][:, kcol[u]]) for u in us]
            st = [st[u] * jnp.exp(tot[i][:, kcol[u]]) + upd[u] for u in us]
            for u in us:
                emit(i * SUB, SUB, u, o_in[u] + o_di[u])
        for u in us:
            s_ref[u] = st[u]


def _gla(proj, lower_bounds, nc, heads, dk, dv, layer, of=None, norm=None):
    reverse = of is not None
    m = proj.shape[0]
    b = m // (nc * CHUNK)
    unit = _tile(heads, (8, 4, 2, 1))
    ng = heads // unit
    depth = lower_bounds.shape[0]
    assert dk == dv

    def row(i, c):
        return i * nc + (nc - 1 - c if reverse else c)

    in_specs = [
        pl.BlockSpec((CHUNK, unit * dk), lambda i, g, c: (row(i, c), g)),
        pl.BlockSpec((CHUNK, unit * dk), lambda i, g, c: (row(i, c), (2 if reverse else 1) * ng + g)),
        pl.BlockSpec((CHUNK, unit * dv), lambda i, g, c: (row(i, c), 3 * ng + g)),
        pl.BlockSpec((depth, unit * dk), lambda i, g, c: (0, g)),
    ]
    args = [proj, proj, proj, lower_bounds.astype(F32)]
    if reverse:
        in_specs += [
            pl.BlockSpec((CHUNK, unit * dv), lambda i, g, c: (row(i, c), g)),
            pl.BlockSpec((CHUNK, unit * dv), lambda i, g, c: (row(i, c), 4 * ng + g)),
            pl.BlockSpec((1, dv), lambda i, g, c: (0, 0)),
        ]
        args += [of, proj, norm.reshape(1, dv).astype(F32)]
    return pl.pallas_call(
        functools.partial(_gla_kernel, unit=unit, dk=dk, dv=dv, layer=layer, reverse=reverse),
        out_shape=jax.ShapeDtypeStruct((m, heads * dv), BF16 if reverse else F32),
        grid=(b, ng, nc),
        in_specs=in_specs,
        out_specs=pl.BlockSpec((CHUNK, unit * dv), lambda i, g, c: (row(i, c), g)),
        scratch_shapes=[pltpu.VMEM((unit, dv, dk), F32)],
        compiler_params=_cp(("parallel", "parallel", "arbitrary")),
        name="hgrn2_bwd" if reverse else "hgrn2_fwd",
    )(*args)


def _diff_attn_kernel(q_ref, k_ref, v_ref, qn_ref, kn_ref, lam_ref, sn_ref, o_ref, kn_s, v_s, *, hd, heads, lam_init):
    h = pl.program_id(1)
    qi = pl.program_id(2)
    tq = q_ref.shape[0]
    t = k_ref.shape[0]

    @pl.when(qi == 0)
    def _():
        kk = k_ref[...]
        for mp in range(2):
            km = kk[:, mp * hd:(mp + 1) * hd]
            ms = jnp.mean(km * km, axis=1, keepdims=True)
            kn_s[mp] = (km * lax.rsqrt(ms + EPS) * kn_ref[...]).astype(BF16)
        v_s[...] = v_ref[...].astype(BF16)

    lv = lam_ref[...]
    lam = (jnp.exp(jnp.sum(lv[0:1] * lv[1:2], axis=1, keepdims=True))
           - jnp.exp(jnp.sum(lv[2:3] * lv[3:4], axis=1, keepdims=True)) + lam_init)
    slope2 = jnp.exp2(-8.0 * (jnp.full((1, 1), h + 1, jnp.int32).astype(F32)) / heads) * LOG2E
    kpos = _iota((1, t), 1)
    kbias = jnp.where(kpos >= PAD, slope2 * kpos.astype(F32), NEG)
    rb = _tile(tq, (88, 64, 32, 16, 8))
    blocks = range(0, tq, rb)
    ridx = [qi * tq + r0 + _iota((rb, 1), 0) for r0 in blocks]
    nb = len(ridx)
    s = []
    for r0 in blocks:
        q = q_ref[r0:r0 + rb, :]
        for mp in range(2):
            qm = q[:, mp * hd:(mp + 1) * hd]
            ms = jnp.mean(qm * qm, axis=1, keepdims=True)
            qn = qm * lax.rsqrt(ms + EPS) * qn_ref[...]
            s.append(lax.dot_general(qn.astype(BF16), kn_s[mp], (((1,), (1,)), ((), ())),
                                     preferred_element_type=F32))
    a = []
    for i in range(nb):
        bias2 = jnp.abs(slope2 * ridx[i].astype(F32) - kbias)
        es, inv = [], []
        for mp in range(2):
            x = s[2 * i + mp] * (hd ** -0.5 * LOG2E) - bias2
            e = jnp.exp2(x - jnp.max(x, axis=1, keepdims=True))
            es.append(e)
            inv.append(1.0 / jnp.sum(e, axis=1, keepdims=True))
        a.append((es[0] * inv[0] - es[1] * (lam * inv[1])).astype(BF16))
    o = [jnp.dot(a[i], v_s[...], preferred_element_type=F32) for i in range(nb)]
    for i, r0 in enumerate(blocks):
        ms = jnp.mean(o[i] * o[i], axis=1, keepdims=True)
        y = o[i] * lax.rsqrt(ms + EPS) * sn_ref[...] * (1.0 - lam_init)
        o_ref[r0:r0 + rb, :] = jnp.where(ridx[i] >= PAD, y, 0.0).astype(o_ref.dtype)


def _diff_attn(proj, col_off, t, heads, hd, q_norm, k_norm, lam_vec, sub_norm, lam_init):
    m = proj.shape[0]
    b = m // t
    w = 2 * hd
    tq = _tile(t, (528, 264, 192, 64))
    nq = t // tq
    c0 = col_off // w
    return pl.pallas_call(
        functools.partial(_diff_attn_kernel, hd=hd, heads=heads, lam_init=lam_init),
        out_shape=jax.ShapeDtypeStruct((m, heads * w), BF16),
        grid=(b, heads, nq),
        in_specs=[
            pl.BlockSpec((tq, w), lambda i, h, j: (i * nq + j, c0 + h)),
            pl.BlockSpec((t, w), lambda i, h, j: (i, c0 + heads + h)),
            pl.BlockSpec((t, w), lambda i, h, j: (i, c0 + 2 * heads + h)),
            pl.BlockSpec((1, hd), lambda i, h, j: (0, 0)),
            pl.BlockSpec((1, hd), lambda i, h, j: (0, 0)),
            pl.BlockSpec((4, hd), lambda i, h, j: (0, 0)),
            pl.BlockSpec((1, w), lambda i, h, j: (0, 0)),
        ],
        out_specs=pl.BlockSpec((tq, w), lambda i, h, j: (i * nq + j, h)),
        scratch_shapes=[pltpu.VMEM((2, t, hd), BF16), pltpu.VMEM((t, w), BF16)],
        compiler_params=_cp(("parallel", "parallel", "arbitrary")),
        name="diff_attn",
    )(proj, proj, proj, q_norm.reshape(1, hd).astype(F32), k_norm.reshape(1, hd).astype(F32),
      lam_vec.astype(F32), sub_norm.reshape(1, w).astype(F32))


def _place(width, pieces):
    row = jnp.zeros((width,), F32)
    for off, vals in pieces:
        row = lax.dynamic_update_slice(row, vals.astype(F32).reshape(-1), (off,))
    return row.reshape(1, width)


def _trunk(h, t, p, bp):
    m, d = h.shape
    nc = t // CHUNK
    mix_w = d // 2
    depth = p["norm_mix"].shape[0]
    ha = p["mlstm_i_bias"].shape[-1]
    a_dv = mix_w // ha
    a_dqk = a_dv // 2
    hb = p["dn_a_log"].shape[-1]
    b_dk = mix_w // hb
    c_dv = p["hgrn_norm"].shape[-1]
    hc = mix_w // c_dv
    c_dk = p["hgrn_lower_bounds"].shape[-1] // hc
    d_hd = p["diff_q_norm"].shape[-1]
    hdh = mix_w // (2 * d_hd)
    o_ag = 2 * ha * a_dqk + 2 * ha * a_dv
    o_qkv = o_ag + 4 * ha
    o_bz = o_qkv + 3 * mix_w
    o_bg = o_bz + mix_w
    assert 4 * ha + 4 * hb <= LANE
    w_up = p["ffn_w_up"].astype(BF16)
    w_down = p["ffn_w_down"].astype(BF16)
    for l in range(depth):
        hn = _rmsnorm_bf16(h, p["norm_mix"][l])
        if l % 2 == 0:
            e = l // 2
            w_in = p["ab_w_in"][e]
            w_plain = jnp.concatenate([w_in[:, :o_ag], w_in[:, o_bz:o_bg]], axis=1).astype(BF16)
            w_gate = jnp.concatenate(
                [w_in[:, o_ag:o_qkv], w_in[:, o_bg:], jnp.zeros((d, LANE - 4 * ha - 4 * hb), F32)], axis=1).astype(BF16)
            w_qkv = w_in[:, o_qkv:o_bz].astype(BF16)
            proj = _matmul(hn, w_plain, name="ab_proj")
            gates = _matmul(hn, w_gate, name="ab_gates")
            qkv = _qkv_conv(hn, w_qkv, p["dn_conv"][e], t, b_dk)
            bias_row = _place(LANE, [(dd * 2 * ha + gg * ha, (p["mlstm_i_bias"][e], p["mlstm_f_bias"][e])[gg][dd])
                                     for dd in range(2) for gg in range(2)])
            yf = _mlstm(proj, gates, bias_row, nc, ha, a_dqk, a_dv)
            xa = _mlstm(proj, gates, bias_row, nc, ha, a_dqk, a_dv, yf=yf, norm=p["mlstm_norm"][e])
            dt_row = _place(LANE, [(4 * ha + dd * 2 * hb, p["dn_dt_bias"][e][dd]) for dd in range(2)])
            alog_row = _place(LANE, [(4 * ha + dd * 2 * hb, p["dn_a_log"][e][dd]) for dd in range(2)])
            of = _gdn(qkv, gates, dt_row, alog_row, nc, hb, b_dk, b_dk, 4 * ha)
            xb = _gdn(qkv, gates, dt_row, alog_row, nc, hb, b_dk, b_dk, 4 * ha, of=of, z=proj, z_off=o_ag,
                      norm=p["dn_norm"][e])
            w_out = p["ab_w_out"][e].astype(BF16)
        else:
            o = l // 2
            proj = _matmul(hn, p["cd_w_in"][o].astype(BF16), name="cd_proj")
            of = _gla(proj, p["hgrn_lower_bounds"], nc, hc, c_dk, c_dv, l)
            xa = _gla(proj, p["hgrn_lower_bounds"], nc, hc, c_dk, c_dv, l, of=of, norm=p["hgrn_norm"][o])
            lam_init = 0.8 - 0.6 * math.exp(-0.3 * l)
            xb = _diff_attn(proj, 3 * hc * c_dk + 2 * hc * c_dv, t, hdh, d_hd, p["diff_q_norm"][o],
                            p["diff_k_norm"][o], p["diff_lambda"][o], p["diff_norm"][o], lam_init)
            w_out = p["cd_w_out"][o].astype(BF16)
        h = _matmul2_residual(xa, xb, w_out, h)
        hn = _rmsnorm_bf16(h, p["norm_ffn"][l])
        act = _ffn_up(hn, w_up, l, p["ffn_conv_w"][l], p["ffn_conv_b"][l], t)
        if l + 1 < depth:
            h = _matmul_residual(act, w_down, l, h)
    b = m // t
    s = t - PAD - N_META
    y_p = _matmul_residual_tokens(act, w_down, depth - 1, h, t, 0, bp, name="ffn_down_prompt")
    y_s = _matmul_residual_tokens(act, w_down, depth - 1, h, t, bp, b - bp, name="ffn_down_sample")
    return y_p.reshape(bp, s, d), y_s.reshape(b - bp, s, d)


def kernel(x_prompt, x_sample, meta_tokens, norm_mix, norm_ffn, ab_w_in, ab_w_out, mlstm_i_bias, mlstm_f_bias, mlstm_norm, dn_conv, dn_a_log, dn_dt_bias, dn_norm, cd_w_in, cd_w_out, hgrn_lower_bounds, hgrn_norm, diff_q_norm, diff_k_norm, diff_lambda, diff_norm, ffn_w_up, ffn_conv_w, ffn_conv_b, ffn_w_down):
    p = dict(norm_mix=norm_mix, norm_ffn=norm_ffn, ab_w_in=ab_w_in, ab_w_out=ab_w_out, mlstm_i_bias=mlstm_i_bias,
             mlstm_f_bias=mlstm_f_bias, mlstm_norm=mlstm_norm, dn_conv=dn_conv, dn_a_log=dn_a_log,
             dn_dt_bias=dn_dt_bias, dn_norm=dn_norm, cd_w_in=cd_w_in, cd_w_out=cd_w_out,
             hgrn_lower_bounds=hgrn_lower_bounds, hgrn_norm=hgrn_norm, diff_q_norm=diff_q_norm,
             diff_k_norm=diff_k_norm, diff_lambda=diff_lambda, diff_norm=diff_norm, ffn_w_up=ffn_w_up,
             ffn_conv_w=ffn_conv_w, ffn_conv_b=ffn_conv_b, ffn_w_down=ffn_w_down)
    bp, s, d = x_prompt.shape
    bs = x_sample.shape[0]
    assert x_sample.shape[1:] == (s, d)
    t = PAD + N_META + s
    assert t % CHUNK == 0
    b = bp + bs
    front = jnp.concatenate([jnp.zeros((PAD, d), F32), meta_tokens.astype(F32)], axis=0)
    h = jnp.concatenate(
        [jnp.concatenate([jnp.broadcast_to(front[None], (n, PAD + N_META, d)), x.astype(F32)], axis=1)
         for n, x in ((bp, x_prompt), (bs, x_sample))], axis=0).reshape(b * t, d)
    return _trunk(h, t, p, bp)
```

```python
import functools
import math

import jax
import jax.numpy as jnp
from jax import lax
from jax.experimental import pallas as pl
from jax.experimental.pallas import tpu as pltpu

F32 = jnp.float32
BF16 = jnp.bfloat16
EPS = 1e-6
NEG = -1e30
N_META = 16
CHUNK = 64
PAD = CHUNK - N_META
SUB = 16
GLA_SAFE_DECAY = 60.0
LANE = 128
LOG2E = 1.4426950408889634
VMEM_LIMIT = 56 * 1024 * 1024


def _cp(sem, vmem=VMEM_LIMIT):
    return pltpu.CompilerParams(dimension_semantics=sem, vmem_limit_bytes=vmem)


def _iota(shape, dim):
    return lax.broadcasted_iota(jnp.int32, shape, dim)


def _dot(a, b):
    return jnp.dot(a.astype(BF16), b.astype(BF16), preferred_element_type=F32)


def _dot_nt(a, b):
    return lax.dot_general(a.astype(BF16), b.astype(BF16), (((1,), (1,)), ((), ())), preferred_element_type=F32)


def _dot_tn(a, b):
    return lax.dot_general(a.astype(BF16), b.astype(BF16), (((0,), (0,)), ((), ())), preferred_element_type=F32)


def _log_sigmoid(x):
    return jnp.minimum(x, 0.0) - jnp.log1p(jnp.exp(-jnp.abs(x)))


def _softplus(x):
    return jnp.maximum(x, 0.0) + jnp.log1p(jnp.exp(-jnp.abs(x)))


def _sigmoid(x):
    return 1.0 / (1.0 + jnp.exp(-x))


def _silu(x):
    return x * _sigmoid(x)


def _tile(n, cands):
    for c in cands:
        if n % c == 0:
            return c
    return n


def _rms_kernel(x_ref, g_ref, o_ref):
    x = x_ref[...]
    ms = jnp.mean(x * x, axis=-1, keepdims=True)
    o_ref[...] = (x * lax.rsqrt(ms + EPS) * g_ref[...]).astype(o_ref.dtype)


def _rmsnorm_bf16(x, g):
    m, d = x.shape
    tm = _tile(m, (528, 264, 192, 64))
    return pl.pallas_call(
        _rms_kernel,
        out_shape=jax.ShapeDtypeStruct((m, d), BF16),
        grid=(m // tm,),
        in_specs=[pl.BlockSpec((tm, d), lambda i: (i, 0)), pl.BlockSpec((1, d), lambda i: (0, 0))],
        out_specs=pl.BlockSpec((tm, d), lambda i: (i, 0)),
        compiler_params=_cp(("parallel",)),
        name="rmsnorm",
    )(x, g.reshape(1, d).astype(F32))


def _mm_kernel(x_ref, w_ref, o_ref):
    o_ref[...] = jnp.dot(x_ref[...], w_ref[...], preferred_element_type=F32).astype(o_ref.dtype)


def _matmul(x, w, out_dtype=F32, name="matmul"):
    m, k = x.shape
    n = w.shape[1]
    tm = _tile(m, (1056, 704, 192, 64))
    tn = _tile(n, (512, 256, 128))
    return pl.pallas_call(
        _mm_kernel,
        out_shape=jax.ShapeDtypeStruct((m, n), out_dtype),
        grid=(m // tm, n // tn),
        in_specs=[pl.BlockSpec((tm, k), lambda i, j: (i, 0)), pl.BlockSpec((k, tn), lambda i, j: (0, j))],
        out_specs=pl.BlockSpec((tm, tn), lambda i, j: (i, j)),
        compiler_params=_cp(("parallel", "arbitrary")),
        name=name,
    )(x, w)


def _mm2_res_kernel(xa_ref, xb_ref, wa_ref, wb_ref, r_ref, o_ref):
    acc = jnp.dot(xa_ref[...], wa_ref[...], preferred_element_type=F32)
    acc = acc + jnp.dot(xb_ref[...], wb_ref[...], preferred_element_type=F32)
    o_ref[...] = r_ref[...] + acc


def _matmul2_residual(xa, xb, w, res, name="mix_out"):
    m, ka = xa.shape
    kb = xb.shape[1]
    n = w.shape[1]
    tm = _tile(m, (1056, 704, 192, 64))
    tn = _tile(n, (512, 256, 128))
    assert ka == kb
    return pl.pallas_call(
        _mm2_res_kernel,
        out_shape=jax.ShapeDtypeStruct((m, n), F32),
        grid=(m // tm, n // tn),
        in_specs=[
            pl.BlockSpec((tm, ka), lambda i, j: (i, 0)),
            pl.BlockSpec((tm, kb), lambda i, j: (i, 0)),
            pl.BlockSpec((ka, tn), lambda i, j: (0, j)),
            pl.BlockSpec((kb, tn), lambda i, j: (1, j)),
            pl.BlockSpec((tm, tn), lambda i, j: (i, j)),
        ],
        out_specs=pl.BlockSpec((tm, tn), lambda i, j: (i, j)),
        compiler_params=_cp(("parallel", "arbitrary")),
        name=name,
    )(xa, xb, w, w, res)


def _mm_res_kernel(x_ref, w_ref, r_ref, o_ref):
    o_ref[...] = r_ref[...] + jnp.dot(x_ref[...], w_ref[...], preferred_element_type=F32)


def _matmul_residual(x, w, layer, res, name="ffn_down"):
    m, k = x.shape
    n = w.shape[2]
    tm = _tile(m, (704, 192, 64))
    tn = _tile(n, (256, 128))
    return pl.pallas_call(
        _mm_res_kernel,
        out_shape=jax.ShapeDtypeStruct((m, n), F32),
        grid=(m // tm, n // tn),
        in_specs=[
            pl.BlockSpec((tm, k), lambda i, j: (i, 0)),
            pl.BlockSpec((None, k, tn), lambda i, j: (layer, 0, j)),
            pl.BlockSpec((tm, tn), lambda i, j: (i, j)),
        ],
        out_specs=pl.BlockSpec((tm, tn), lambda i, j: (i, j)),
        compiler_params=_cp(("parallel", "arbitrary")),
        name=name,
    )(x, w, res)


def _matmul_residual_tokens(x, w, layer, res, t, seq0, nseq, name="ffn_down_out"):
    m, k = x.shape
    n = w.shape[2]
    s = t - PAD - N_META
    tm = _tile(s, (1024, 512, 128, 64))
    per = s // tm
    tn = _tile(n, (256, 128))

    def row0(a, r):
        return pl.multiple_of((seq0 + a) * t + (PAD + N_META) + r * tm, CHUNK)

    return pl.pallas_call(
        _mm_res_kernel,
        out_shape=jax.ShapeDtypeStruct((nseq * s, n), F32),
        grid=(nseq, per, n // tn),
        in_specs=[
            pl.BlockSpec((pl.Element(tm), pl.Element(k)), lambda a, r, j: (row0(a, r), 0),
                         pipeline_mode=pl.Buffered(1)),
            pl.BlockSpec((None, k, tn), lambda a, r, j: (layer, 0, j)),
            pl.BlockSpec((pl.Element(tm), pl.Element(tn)), lambda a, r, j: (row0(a, r), j * tn)),
        ],
        out_specs=pl.BlockSpec((tm, tn), lambda a, r, j: (a * per + r, j)),
        compiler_params=_cp(("parallel", "parallel", "arbitrary")),
        name=name,
    )(x, w, res)


def _row_chunks(t):
    return _tile(t // 8, (4, 2, 1))


def _conv3_chunk(ys, c, cw):
    nch = len(ys)
    rc = ys[c].shape[0]
    rows = _iota((rc, 1), 0)
    up = pltpu.roll(ys[c], 1, 0)
    if c > 0:
        up = jnp.where(rows == 0, ys[c - 1][rc - 1:rc], up)
    dn = jnp.where(rows == rc - 1, ys[(c + 1) % nch][0:1], pltpu.roll(ys[c], rc - 1, 0))
    return cw[0:1] * up + cw[1:2] * ys[c] + cw[2:3] * dn


def _ffn_up_kernel(x_ref, wg_ref, wv_ref, cw_ref, cb_ref, o_ref):
    t = x_ref.shape[0]
    nch = _row_chunks(t)
    rc = t // nch
    cw = cw_ref[...]
    gs, vs = [], []

    def epilogue(c):
        gc = _conv3_chunk(gs, c, cw) + cb_ref[...]
        o_ref[c * rc:(c + 1) * rc, :] = (_silu(gc) * vs[c]).astype(o_ref.dtype)

    for c in range(nch):
        x = x_ref[c * rc:(c + 1) * rc, :]
        gs.append(jnp.dot(x, wg_ref[...], preferred_element_type=F32))
        vs.append(jnp.dot(x, wv_ref[...], preferred_element_type=F32))
        if c >= 1:
            epilogue(c - 1)
    epilogue(nch - 1)


def _ffn_up(xn, w_up, layer, conv_w, conv_b, t):
    m, d = xn.shape
    f = conv_w.shape[1]
    tn = _tile(f, (256, 128))
    nj = f // tn
    return pl.pallas_call(
        _ffn_up_kernel,
        out_shape=jax.ShapeDtypeStruct((m, f), BF16),
        grid=(m // t, nj),
        in_specs=[
            pl.BlockSpec((t, d), lambda i, j: (i, 0), pipeline_mode=pl.Buffered(1)),
            pl.BlockSpec((None, d, tn), lambda i, j: (layer, 0, j)),
            pl.BlockSpec((None, d, tn), lambda i, j: (layer, 0, j + nj)),
            pl.BlockSpec((3, tn), lambda i, j: (0, j)),
            pl.BlockSpec((1, tn), lambda i, j: (0, j)),
        ],
        out_specs=pl.BlockSpec((t, tn), lambda i, j: (i, j)),
        compiler_params=_cp(("parallel", "arbitrary")),
        name="ffn_up",
    )(xn, w_up, w_up, conv_w.astype(F32), conv_b.reshape(1, f).astype(F32))


def _qkv_conv_kernel(x_ref, w_ref, cw_ref, o_ref, *, nq, qscale):
    j = pl.program_id(1)
    t, tn = o_ref.shape
    nch = _row_chunks(t)
    rc = t // nch
    cw = cw_ref[...]
    is_qk = j < 2 * nq
    scale = jnp.where(j < nq, qscale, 1.0)
    ys = []

    def epilogue(c):
        a = _silu(_conv3_chunk(ys, c, cw))
        a = jnp.where(c * rc + _iota((rc, 1), 0) >= PAD, a, 0.0)
        for s in range(tn // LANE):
            seg = a[:, s * LANE:(s + 1) * LANE]
            ss = jnp.sum(seg * seg, axis=1, keepdims=True)
            nrm = seg * lax.rsqrt(ss + EPS) * scale
            o_ref[c * rc:(c + 1) * rc, s * LANE:(s + 1) * LANE] = jnp.where(is_qk, nrm, seg)

    for c in range(nch):
        ys.append(jnp.dot(x_ref[c * rc:(c + 1) * rc, :], w_ref[...], preferred_element_type=F32))
        if c >= 1:
            epilogue(c - 1)
    epilogue(nch - 1)


def _qkv_conv(xn, w, conv_w, t, dk):
    m, d = xn.shape
    n = w.shape[1]
    tn = _tile(n // 3, (256, 128))
    return pl.pallas_call(
        functools.partial(_qkv_conv_kernel, nq=(n // 3) // tn, qscale=dk ** -0.5),
        out_shape=jax.ShapeDtypeStruct((m, n), F32),
        grid=(m // t, n // tn),
        in_specs=[
            pl.BlockSpec((t, d), lambda i, j: (i, 0), pipeline_mode=pl.Buffered(1)),
            pl.BlockSpec((d, tn), lambda i, j: (0, j)),
            pl.BlockSpec((3, tn), lambda i, j: (0, j)),
        ],
        out_specs=pl.BlockSpec((t, tn), lambda i, j: (i, j)),
        compiler_params=_cp(("parallel", "arbitrary")),
        name="dn_qkv_conv",
    )(xn, w, conv_w.astype(F32))


def _chunk_masks(reverse):
    r = _iota((CHUNK, CHUNK), 0)
    s = _iota((CHUNK, CHUNK), 1)
    tri = (s >= r) if reverse else (s <= r)
    strict = (s > r) if reverse else (s < r)
    return tri, strict, r == s


def _scan_rows(x, seg, reverse):
    rows = x.shape[0]
    r = _iota((rows, 1), 0) % seg
    s = 1
    while s < seg:
        if reverse:
            x = x + jnp.where(r < seg - s, pltpu.roll(x, rows - s, 0), 0.0)
        else:
            x = x + jnp.where(r >= s, pltpu.roll(x, s, 0), 0.0)
        s *= 2
    return x


def _transpose_tile(x):
    return jnp.concatenate([x, jnp.zeros((LANE - x.shape[0], LANE), F32)], axis=0).T


def _valid_rows(chunk):
    return (chunk * CHUNK + _iota((CHUNK, 1), 0)) >= PAD


def _head_norm_gate(y, gain, gate, valid):
    ms = jnp.mean(y * y, axis=1, keepdims=True)
    return jnp.where(valid, y * lax.rsqrt(ms + EPS) * gain * gate, 0.0)


def _mlstm_kernel(*refs, nseq, heads, dk, dv, reverse):
    if reverse:
        q_ref, k_ref, v_ref, g_ref, bias_ref, yf_ref, ao_ref, nrm_ref, o_ref, c_ref, n_ref, m_ref = refs
    else:
        q_ref, k_ref, v_ref, g_ref, bias_ref, o_ref, c_ref, n_ref, m_ref = refs
    step = pl.program_id(1)
    chunk = pl.num_programs(1) - 1 - step if reverse else step

    @pl.when(step == 0)
    def _():
        c_ref[...] = jnp.zeros_like(c_ref)
        n_ref[...] = jnp.zeros_like(n_ref)
        m_ref[...] = jnp.full_like(m_ref, NEG)

    valid = _valid_rows(chunk)
    tri, _, _ = _chunk_masks(reverse)
    last = 0 if reverse else CHUNK - 1
    base = 2 * heads if reverse else 0
    i_all, cum, i_t, cum_t = [], [], [], []
    for sq in range(nseq):
        gt = g_ref[sq] + bias_ref[...]
        i_all.append(jnp.where(valid, gt, NEG))
        cum.append(_scan_rows(jnp.where(valid, _log_sigmoid(gt), 0.0), CHUNK, reverse))
        i_t.append(_transpose_tile(i_all[sq]))
        cum_t.append(_transpose_tile(cum[sq]))
    units = [(sq, h) for sq in range(nseq) for h in range(heads)]
    us = range(len(units))
    ci = [base + h for _, h in units]
    cf = [base + heads + h for _, h in units]
    i_col = [i_all[sq][:, ci[u]:ci[u] + 1] for u, (sq, _) in enumerate(units)]
    i_row = [i_t[sq][ci[u]:ci[u] + 1, :CHUNK] for u, (sq, _) in enumerate(units)]
    b_col = [cum[sq][:, cf[u]:cf[u] + 1] for u, (sq, _) in enumerate(units)]
    b_row = [cum_t[sq][cf[u]:cf[u] + 1, :CHUNK] for u, (sq, _) in enumerate(units)]
    b_last = [cum[sq][last:last + 1, cf[u]:cf[u] + 1] for u, (sq, _) in enumerate(units)]
    q = [q_ref[sq, :, h * dk:(h + 1) * dk] for sq, h in units]
    k = [k_ref[sq, :, h * dk:(h + 1) * dk] * (dk ** -0.5) for sq, h in units]
    v = [v_ref[sq, :, h * dv:(h + 1) * dv] for sq, h in units]
    ct = [c_ref[u] for u in us]
    n_row = [n_ref[u] for u in us]
    m = [m_ref[u][:, 0:1] for u in us]
    qk = [_dot_nt(q[u], k[u]) for u in us]
    qc = [_dot(q[u], ct[u]) for u in us]
    dmat = [jnp.where(tri, b_col[u] - b_row[u] + i_row[u], NEG) for u in us]
    inter = [b_col[u] + m[u] for u in us]
    m_t = [jnp.maximum(inter[u], jnp.max(dmat[u], axis=1, keepdims=True)) for u in us]
    w = [jnp.exp(dmat[u] - m_t[u]) * qk[u] for u in us]
    s_inter = [jnp.exp(inter[u] - m_t[u]) for u in us]
    wv = [_dot(w[u], v[u]) for u in us]
    dec = [b_last[u] - b_col[u] + i_col[u] for u in us]
    m_new = [jnp.maximum(b_last[u] + m[u], jnp.max(dec[u], axis=0, keepdims=True)) for u in us]
    a_old = [jnp.exp(b_last[u] + m[u] - m_new[u]) for u in us]
    w_in = [jnp.exp(dec[u] - m_new[u]) for u in us]
    kv = [_dot_tn(k[u], w_in[u] * v[u]) for u in us]
    for u, (sq, h) in enumerate(units):
        num = s_inter[u] * qc[u] + wv[u]
        den = (s_inter[u] * jnp.sum(q[u] * n_row[u], axis=1, keepdims=True)
               + jnp.sum(w[u], axis=1, keepdims=True))
        hout = num / jnp.maximum(jnp.abs(den), jnp.exp(-m_t[u]))
        c_ref[u] = a_old[u] * ct[u] + kv[u]
        n_ref[u] = a_old[u] * n_row[u] + jnp.sum(w_in[u] * k[u], axis=0, keepdims=True)
        m_ref[u] = jnp.broadcast_to(m_new[u], (1, LANE))
        cols = slice(h * dv, (h + 1) * dv)
        if reverse:
            y = yf_ref[sq, :, cols] + hout
            o_ref[sq, :, cols] = _head_norm_gate(
                y, nrm_ref[:, cols], _sigmoid(ao_ref[sq, :, cols]), valid).astype(o_ref.dtype)
        else:
            o_ref[sq, :, cols] = hout


def _mlstm(proj, gates, bias_row, nc, heads, dk, dv, yf=None, norm=None):
    reverse = yf is not None
    m = proj.shape[0]
    t = nc * CHUNK
    b = m // t
    wq, wv = heads * dk, heads * dv
    nseq = _tile(b, (4, 3, 2, 1))
    proj3 = proj.reshape(b, t, proj.shape[1])
    gates3 = gates.reshape(b, t, LANE)

    def ch(c):
        return nc - 1 - c if reverse else c

    in_specs = [
        pl.BlockSpec((nseq, CHUNK, wq), lambda i, c: (i, ch(c), 0)),
        pl.BlockSpec((nseq, CHUNK, wq), lambda i, c: (i, ch(c), 1)),
        pl.BlockSpec((nseq, CHUNK, wv), lambda i, c: (i, ch(c), (2 * wq) // wv)),
        pl.BlockSpec((nseq, CHUNK, LANE), lambda i, c: (i, ch(c), 0)),
        pl.BlockSpec((1, LANE), lambda i, c: (0, 0)),
    ]
    args = [proj3, proj3, proj3, gates3, bias_row]
    if reverse:
        in_specs += [
            pl.BlockSpec((nseq, CHUNK, wv), lambda i, c: (i, ch(c), 0)),
            pl.BlockSpec((nseq, CHUNK, wv), lambda i, c: (i, ch(c), (2 * wq) // wv + 1)),
            pl.BlockSpec((1, wv), lambda i, c: (0, 0)),
        ]
        args += [yf.reshape(b, t, wv), proj3, norm.reshape(1, wv).astype(F32)]
    out = pl.pallas_call(
        functools.partial(_mlstm_kernel, nseq=nseq, heads=heads, dk=dk, dv=dv, reverse=reverse),
        out_shape=jax.ShapeDtypeStruct((b, t, wv), BF16 if reverse else F32),
        grid=(b // nseq, nc),
        in_specs=in_specs,
        out_specs=pl.BlockSpec((nseq, CHUNK, wv), lambda i, c: (i, ch(c), 0)),
        scratch_shapes=[
            pltpu.VMEM((nseq * heads, dk, dv), F32),
            pltpu.VMEM((nseq * heads, 1, dk), F32),
            pltpu.VMEM((nseq * heads, 1, LANE), F32),
        ],
        compiler_params=_cp(("parallel", "arbitrary")),
        name="mlstm_bwd" if reverse else "mlstm_fwd",
    )(*args)
    return out.reshape(m, wv)


def _gdn_kernel(*refs, nseq, heads, dk, dv, gate_off, reverse):
    if reverse:
        q_ref, k_ref, v_ref, g_ref, dt_ref, alog_ref, of_ref, z_ref, nrm_ref, o_ref, s_ref = refs
    else:
        q_ref, k_ref, v_ref, g_ref, dt_ref, alog_ref, o_ref, s_ref = refs
    step = pl.program_id(1)
    chunk = pl.num_programs(1) - 1 - step if reverse else step

    @pl.when(step == 0)
    def _():
        s_ref[...] = jnp.zeros_like(s_ref)

    valid = _valid_rows(chunk)
    tri, strict, eye = _chunk_masks(reverse)
    eye_f = jnp.where(eye, 1.0, 0.0)
    last = 0 if reverse else CHUNK - 1
    base = gate_off + (2 * heads if reverse else 0)
    cum, cum_t, beta_all = [], [], []
    for sq in range(nseq):
        raw = g_ref[sq]
        decay = jnp.where(valid, -jnp.exp(alog_ref[...]) * _softplus(raw + dt_ref[...]), 0.0)
        beta_all.append(jnp.where(valid, _sigmoid(raw), 0.0))
        cum.append(_scan_rows(decay, CHUNK, reverse))
        cum_t.append(_transpose_tile(cum[sq]))
    units = [(sq, h) for sq in range(nseq) for h in range(heads)]
    us = range(len(units))
    b_col = [cum[sq][:, base + h:base + h + 1] for sq, h in units]
    b_row = [cum_t[sq][base + h:base + h + 1, :CHUNK] for sq, h in units]
    b_last = [cum[sq][last:last + 1, base + h:base + h + 1] for sq, h in units]
    beta = [beta_all[sq][:, base + heads + h:base + heads + h + 1] for sq, h in units]
    q = [q_ref[sq, :, h * dk:(h + 1) * dk] for sq, h in units]
    k = [k_ref[sq, :, h * dk:(h + 1) * dk] for sq, h in units]
    v = [v_ref[sq, :, h * dv:(h + 1) * dv] for sq, h in units]
    s = [s_ref[u] for u in us]
    kb = [k[u] * beta[u] for u in us]
    kk = [_dot_nt(kb[u], k[u]) for u in us]
    qk = [_dot_nt(q[u], k[u]) for u in us]
    decay = [jnp.exp(jnp.where(tri, b_col[u] - b_row[u], NEG)) for u in us]
    eb = [jnp.exp(b_col[u]) for u in us]
    qs = [_dot(q[u] * eb[u], s[u]) for u in us]
    x = [jnp.where(strict, kk[u] * decay[u], 0.0) for u in us]
    rhs = [jnp.concatenate([v[u] * beta[u], kb[u] * eb[u]], axis=1) for u in us]
    p = [eye_f - x[u] for u in us]
    n = 1
    while 2 * n < CHUNK:
        x = [_dot(x[u], x[u]) for u in us]
        p = [p[u] + _dot(p[u], x[u]) for u in us]
        n *= 2
    sol = [_dot(p[u], rhs[u]) for u in us]
    ws = [_dot(sol[u][:, dv:], s[u]) for u in us]
    v_new = [sol[u][:, :dv] - ws[u] for u in us]
    attn = [qk[u] * decay[u] for u in us]
    av = [_dot(attn[u], v_new[u]) for u in us]
    kv = [_dot_tn(k[u] * jnp.exp(b_last[u] - b_col[u]), v_new[u]) for u in us]
    for u, (sq, h) in enumerate(units):
        o = qs[u] + av[u]
        s_ref[u] = s[u] * jnp.exp(b_last[u]) + kv[u]
        cols = slice(h * dv, (h + 1) * dv)
        if reverse:
            y = of_ref[sq, :, cols] + o
            o_ref[sq, :, cols] = _head_norm_gate(
                y, nrm_ref[...], _silu(z_ref[sq, :, cols]), valid).astype(o_ref.dtype)
        else:
            o_ref[sq, :, cols] = o


def _gdn(qkv, gates, dt_row, alog_row, nc, heads, dk, dv, gate_off, of=None, z=None, z_off=0, norm=None):
    reverse = of is not None
    m = qkv.shape[0]
    t = nc * CHUNK
    b = m // t
    wk, wv = heads * dk, heads * dv
    nseq = _tile(b, (2, 1))
    qkv3 = qkv.reshape(b, t, qkv.shape[1])

    def ch(c):
        return nc - 1 - c if reverse else c

    in_specs = [
        pl.BlockSpec((nseq, CHUNK, wk), lambda i, c: (i, ch(c), 0)),
        pl.BlockSpec((nseq, CHUNK, wk), lambda i, c: (i, ch(c), 1)),
        pl.BlockSpec((nseq, CHUNK, wv), lambda i, c: (i, ch(c), (2 * wk) // wv)),
        pl.BlockSpec((nseq, CHUNK, LANE), lambda i, c: (i, ch(c), 0)),
        pl.BlockSpec((1, LANE), lambda i, c: (0, 0)),
        pl.BlockSpec((1, LANE), lambda i, c: (0, 0)),
    ]
    args = [qkv3, qkv3, qkv3, gates.reshape(b, t, LANE), dt_row, alog_row]
    if reverse:
        in_specs += [
            pl.BlockSpec((nseq, CHUNK, wv), lambda i, c: (i, ch(c), 0)),
            pl.BlockSpec((nseq, CHUNK, wv), lambda i, c: (i, ch(c), z_off // wv)),
            pl.BlockSpec((1, dv), lambda i, c: (0, 0)),
        ]
        args += [of.reshape(b, t, wv), z.reshape(b, t, z.shape[1]), norm.reshape(1, dv).astype(F32)]
    out = pl.pallas_call(
        functools.partial(_gdn_kernel, nseq=nseq, heads=heads, dk=dk, dv=dv, gate_off=gate_off, reverse=reverse),
        out_shape=jax.ShapeDtypeStruct((b, t, wv), BF16 if reverse else F32),
        grid=(b // nseq, nc),
        in_specs=in_specs,
        out_specs=pl.BlockSpec((nseq, CHUNK, wv), lambda i, c: (i, ch(c), 0)),
        scratch_shapes=[pltpu.VMEM((nseq * heads, dk, dv), F32)],
        compiler_params=_cp(("parallel", "arbitrary")),
        name="gdn_bwd" if reverse else "gdn_fwd",
    )(*args)
    return out.reshape(m, wv)


def _gla_kernel(*refs, unit, dk, dv, layer, reverse):
    if reverse:
        q_ref, f_ref, v_ref, lb_ref, of_ref, z_ref, nrm_ref, o_ref, s_ref = refs
    else:
        q_ref, f_ref, v_ref, lb_ref, o_ref, s_ref = refs
    step = pl.program_id(2)
    chunk = pl.num_programs(2) - 1 - step if reverse else step

    @pl.when(step == 0)
    def _():
        s_ref[...] = jnp.zeros_like(s_ref)

    valid = _valid_rows(chunk)
    lbm = lb_ref[...]
    e = jnp.exp(lbm - jnp.max(lbm, axis=0, keepdims=True))
    lb = jnp.sum(e[1:layer + 1], axis=0, keepdims=True) / jnp.sum(e, axis=0, keepdims=True)
    us = range(unit)
    kcol = [slice(u * dk, (u + 1) * dk) for u in us]
    vcol = [slice(u * dv, (u + 1) * dv) for u in us]
    logf, key, q, cs_all = [], [], [], []
    for u in us:
        x = f_ref[:, kcol[u]]
        lbu = lb[:, kcol[u]]
        z = jnp.exp(-jnp.abs(x))
        r = 1.0 / (1.0 + z)
        f = lbu + (1.0 - lbu) * (jnp.where(x >= 0.0, 1.0, z) * r)
        key.append(jnp.where(valid, (1.0 - lbu) * (jnp.where(x >= 0.0, z, 1.0) * r), 0.0))
        logf.append(jnp.where(valid, jnp.where(f > 0.0, jnp.log(f), x), 0.0))
        q.append(_silu(q_ref[:, kcol[u]]))
        cs_all.append(_scan_rows(logf[u], CHUNK, reverse))
    v = v_ref[...]

    def emit(r0, nrows, u, o):
        rs = slice(r0, r0 + nrows)
        if reverse:
            y = of_ref[rs, vcol[u]] + o
            ok = (chunk * CHUNK + r0 + _iota((nrows, 1), 0)) >= PAD
            o_ref[rs, vcol[u]] = _head_norm_gate(y, nrm_ref[...], _silu(z_ref[rs, vcol[u]]), ok).astype(o_ref.dtype)
        else:
            o_ref[rs, vcol[u]] = o

    lowest = jnp.min(cs_all[0])
    for u in us[1:]:
        lowest = jnp.minimum(lowest, jnp.min(cs_all[u]))
    in_range = lowest >= -GLA_SAFE_DECAY

    @pl.when(in_range)
    def _():
        tri, _, _ = _chunk_masks(reverse)
        last = 0 if reverse else CHUNK - 1
        ecs = [jnp.exp(cs_all[u]) for u in us]
        qe = [q[u] * ecs[u] for u in us]
        kd = [key[u] / ecs[u] for u in us]
        tot = [jnp.exp(cs_all[u][last:last + 1]) for u in us]
        st = [s_ref[u] for u in us]
        sc = [_dot_nt(qe[u], kd[u]) for u in us]
        o_in = [_dot_nt(qe[u], st[u]) for u in us]
        upd = [_dot_tn(v[:, vcol[u]], kd[u] * tot[u]) for u in us]
        o_di = [_dot(jnp.where(tri, sc[u], 0.0), v[:, vcol[u]]) for u in us]
        for u in us:
            s_ref[u] = st[u] * tot[u] + upd[u]
            emit(0, CHUNK, u, o_in[u] + o_di[u])

    @pl.when(jnp.logical_not(in_range))
    def _():
        logf_w = jnp.concatenate(logf, axis=1)
        key_w = jnp.concatenate(key, axis=1)
        q_w = jnp.concatenate(q, axis=1)
        cs = _scan_rows(logf_w, SUB, reverse)
        qe = q_w * jnp.exp(cs)
        nsub = CHUNK // SUB
        s16 = _iota((SUB, LANE), 0)
        t16 = _iota((SUB, LANE), 1)
        keep = (s16 >= t16) if reverse else (s16 <= t16)
        last = 0 if reverse else SUB - 1
        order = list(range(nsub - 1, -1, -1) if reverse else range(nsub))
        rows = [slice(i * SUB, (i + 1) * SUB) for i in range(nsub)]
        attn_t = []
        for i in range(nsub):
            cs_i, key_i, q_i = cs[rows[i]], key_w[rows[i]], q_w[rows[i]]
            acc = [jnp.zeros((SUB, LANE), F32) for _ in us]
            for t in range(SUB):
                p = (q_i[t:t + 1] * key_i) * jnp.exp(jnp.minimum(cs_i[t:t + 1] - cs_i, 0.0))
                for u in us:
                    acc[u] = jnp.where(t16 == t, jnp.sum(p[:, kcol[u]], axis=1, keepdims=True), acc[u])
            attn_t.append([jnp.where(keep, acc[u], 0.0) for u in us])
        tot = [cs[i * SUB + last:i * SUB + last + 1] for i in range(nsub)]
        kd = [key_w[rows[i]] * jnp.exp(tot[i] - cs[rows[i]]) for i in range(nsub)]
        st = [s_ref[u] for u in us]
        for i in order:
            o_in = [_dot_nt(qe[rows[i], kcol[u]], st[u]) for u in us]
            o_di = [_dot_tn(attn_t[i][u], v[rows[i], vcol[u]])[:SUB] for u in us]
            upd = [_dot_tn(v[rows[i], vcol[u]], kd[i][:, kcol[u]]) for u in us]
            st = [st[u] * jnp.exp(tot[i][:, kcol[u]]) + upd[u] for u in us]
            for u in us:
                emit(i * SUB, SUB, u, o_in[u] + o_di[u])
        for u in us:
            s_ref[u] = st[u]


def _gla(proj, lower_bounds, nc, heads, dk, dv, layer, of=None, norm=None):
    reverse = of is not None
    m = proj.shape[0]
    b = m // (nc * CHUNK)
    unit = _tile(heads, (8, 4, 2, 1))
    ng = heads // unit
    depth = lower_bounds.shape[0]
    assert dk == dv

    def row(i, c):
        return i * nc + (nc - 1 - c if reverse else c)

    in_specs = [
        pl.BlockSpec((CHUNK, unit * dk), lambda i, g, c: (row(i, c), g)),
        pl.BlockSpec((CHUNK, unit * dk), lambda i, g, c: (row(i, c), (2 if reverse else 1) * ng + g)),
        pl.BlockSpec((CHUNK, unit * dv), lambda i, g, c: (row(i, c), 3 * ng + g)),
        pl.BlockSpec((depth, unit * dk), lambda i, g, c: (0, g)),
    ]
    args = [proj, proj, proj, lower_bounds.astype(F32)]
    if reverse:
        in_specs += [
            pl.BlockSpec((CHUNK, unit * dv), lambda i, g, c: (row(i, c), g)),
            pl.BlockSpec((CHUNK, unit * dv), lambda i, g, c: (row(i, c), 4 * ng + g)),
            pl.BlockSpec((1, dv), lambda i, g, c: (0, 0)),
        ]
        args += [of, proj, norm.reshape(1, dv).astype(F32)]
    return pl.pallas_call(
        functools.partial(_gla_kernel, unit=unit, dk=dk, dv=dv, layer=layer, reverse=reverse),
        out_shape=jax.ShapeDtypeStruct((m, heads * dv), BF16 if reverse else F32),
        grid=(b, ng, nc),
        in_specs=in_specs,
        out_specs=pl.BlockSpec((CHUNK, unit * dv), lambda i, g, c: (row(i, c), g)),
        scratch_shapes=[pltpu.VMEM((unit, dv, dk), F32)],
        compiler_params=_cp(("parallel", "parallel", "arbitrary")),
        name="hgrn2_bwd" if reverse else "hgrn2_fwd",
    )(*args)


def _diff_attn_kernel(q_ref, k_ref, v_ref, qn_ref, kn_ref, lam_ref, sn_ref, o_ref, kn_s, v_s, *, hd, heads, lam_init):
    h = pl.program_id(1)
    qi = pl.program_id(2)
    tq = q_ref.shape[0]
    t = k_ref.shape[0]

    @pl.when(qi == 0)
    def _():
        kk = k_ref[...]
        for mp in range(2):
            km = kk[:, mp * hd:(mp + 1) * hd]
            ms = jnp.mean(km * km, axis=1, keepdims=True)
            kn_s[mp] = (km * lax.rsqrt(ms + EPS) * kn_ref[...]).astype(BF16)
        v_s[...] = v_ref[...].astype(BF16)

    lv = lam_ref[...]
    lam = (jnp.exp(jnp.sum(lv[0:1] * lv[1:2], axis=1, keepdims=True))
           - jnp.exp(jnp.sum(lv[2:3] * lv[3:4], axis=1, keepdims=True)) + lam_init)
    slope2 = jnp.exp2(-8.0 * (jnp.full((1, 1), h + 1, jnp.int32).astype(F32)) / heads) * LOG2E
    kpos = _iota((1, t), 1)
    kbias = jnp.where(kpos >= PAD, slope2 * kpos.astype(F32), NEG)
    rb = _tile(tq, (88, 64, 32, 16, 8))
    blocks = range(0, tq, rb)
    ridx = [qi * tq + r0 + _iota((rb, 1), 0) for r0 in blocks]
    nb = len(ridx)
    s = []
    for r0 in blocks:
        q = q_ref[r0:r0 + rb, :]
        for mp in range(2):
            qm = q[:, mp * hd:(mp + 1) * hd]
            ms = jnp.mean(qm * qm, axis=1, keepdims=True)
            qn = qm * lax.rsqrt(ms + EPS) * qn_ref[...]
            s.append(lax.dot_general(qn.astype(BF16), kn_s[mp], (((1,), (1,)), ((), ())),
                                     preferred_element_type=F32))
    a = []
    for i in range(nb):
        bias2 = jnp.abs(slope2 * ridx[i].astype(F32) - kbias)
        es, inv = [], []
        for mp in range(2):
            x = s[2 * i + mp] * (hd ** -0.5 * LOG2E) - bias2
            e = jnp.exp2(x - jnp.max(x, axis=1, keepdims=True))
            es.append(e)
            inv.append(1.0 / jnp.sum(e, axis=1, keepdims=True))
        a.append((es[0] * inv[0] - es[1] * (lam * inv[1])).astype(BF16))
    o = [jnp.dot(a[i], v_s[...], preferred_element_type=F32) for i in range(nb)]
    for i, r0 in enumerate(blocks):
        ms = jnp.mean(o[i] * o[i], axis=1, keepdims=True)
        y = o[i] * lax.rsqrt(ms + EPS) * sn_ref[...] * (1.0 - lam_init)
        o_ref[r0:r0 + rb, :] = jnp.where(ridx[i] >= PAD, y, 0.0).astype(o_ref.dtype)


def _diff_attn(proj, col_off, t, heads, hd, q_norm, k_norm, lam_vec, sub_norm, lam_init):
    m = proj.shape[0]
    b = m // t
    w = 2 * hd
    tq = _tile(t, (1056, 528, 264, 192, 64))
    nq = t // tq
    c0 = col_off // w
    return pl.pallas_call(
        functools.partial(_diff_attn_kernel, hd=hd, heads=heads, lam_init=lam_init),
        out_shape=jax.ShapeDtypeStruct((m, heads * w), BF16),
        grid=(b, heads, nq),
        in_specs=[
            pl.BlockSpec((tq, w), lambda i, h, j: (i * nq + j, c0 + h)),
            pl.BlockSpec((t, w), lambda i, h, j: (i, c0 + heads + h)),
            pl.BlockSpec((t, w), lambda i, h, j: (i, c0 + 2 * heads + h)),
            pl.BlockSpec((1, hd), lambda i, h, j: (0, 0)),
            pl.BlockSpec((1, hd), lambda i, h, j: (0, 0)),
            pl.BlockSpec((4, hd), lambda i, h, j: (0, 0)),
            pl.BlockSpec((1, w), lambda i, h, j: (0, 0)),
        ],
        out_specs=pl.BlockSpec((tq, w), lambda i, h, j: (i * nq + j, h)),
        scratch_shapes=[pltpu.VMEM((2, t, hd), BF16), pltpu.VMEM((t, w), BF16)],
        compiler_params=_cp(("parallel", "parallel", "arbitrary")),
        name="diff_attn",
    )(proj, proj, proj, q_norm.reshape(1, hd).astype(F32), k_norm.reshape(1, hd).astype(F32),
      lam_vec.astype(F32), sub_norm.reshape(1, w).astype(F32))


def _place(width, pieces):
    row = jnp.zeros((width,), F32)
    for off, vals in pieces:
        row = lax.dynamic_update_slice(row, vals.astype(F32).reshape(-1), (off,))
    return row.reshape(1, width)


def _trunk(h, t, p, bp):
    m, d = h.shape
    nc = t // CHUNK
    mix_w = d // 2
    depth = p["norm_mix"].shape[0]
    ha = p["mlstm_i_bias"].shape[-1]
    a_dv = mix_w // ha
    a_dqk = a_dv // 2
    hb = p["dn_a_log"].shape[-1]
    b_dk = mix_w // hb
    c_dv = p["hgrn_norm"].shape[-1]
    hc = mix_w // c_dv
    c_dk = p["hgrn_lower_bounds"].shape[-1] // hc
    d_hd = p["diff_q_norm"].shape[-1]
    hdh = mix_w // (2 * d_hd)
    o_ag = 2 * ha * a_dqk + 2 * ha * a_dv
    o_qkv = o_ag + 4 * ha
    o_bz = o_qkv + 3 * mix_w
    o_bg = o_bz + mix_w
    assert 4 * ha + 4 * hb <= LANE
    w_up = p["ffn_w_up"].astype(BF16)
    w_down = p["ffn_w_down"].astype(BF16)
    for l in range(depth):
        hn = _rmsnorm_bf16(h, p["norm_mix"][l])
        if l % 2 == 0:
            e = l // 2
            w_in = p["ab_w_in"][e]
            w_plain = jnp.concatenate([w_in[:, :o_ag], w_in[:, o_bz:o_bg]], axis=1).astype(BF16)
            w_gate = jnp.concatenate(
                [w_in[:, o_ag:o_qkv], w_in[:, o_bg:], jnp.zeros((d, LANE - 4 * ha - 4 * hb), F32)], axis=1).astype(BF16)
            w_qkv = w_in[:, o_qkv:o_bz].astype(BF16)
            proj = _matmul(hn, w_plain, name="ab_proj")
            gates = _matmul(hn, w_gate, name="ab_gates")
            qkv = _qkv_conv(hn, w_qkv, p["dn_conv"][e], t, b_dk)
            bias_row = _place(LANE, [(dd * 2 * ha + gg * ha, (p["mlstm_i_bias"][e], p["mlstm_f_bias"][e])[gg][dd])
                                     for dd in range(2) for gg in range(2)])
            yf = _mlstm(proj, gates, bias_row, nc, ha, a_dqk, a_dv)
            xa = _mlstm(proj, gates, bias_row, nc, ha, a_dqk, a_dv, yf=yf, norm=p["mlstm_norm"][e])
            dt_row = _place(LANE, [(4 * ha + dd * 2 * hb, p["dn_dt_bias"][e][dd]) for dd in range(2)])
            alog_row = _place(LANE, [(4 * ha + dd * 2 * hb, p["dn_a_log"][e][dd]) for dd in range(2)])
            of = _gdn(qkv, gates, dt_row, alog_row, nc, hb, b_dk, b_dk, 4 * ha)
            xb = _gdn(qkv, gates, dt_row, alog_row, nc, hb, b_dk, b_dk, 4 * ha, of=of, z=proj, z_off=o_ag,
                      norm=p["dn_norm"][e])
            w_out = p["ab_w_out"][e].astype(BF16)
        else:
            o = l // 2
            proj = _matmul(hn, p["cd_w_in"][o].astype(BF16), name="cd_proj")
            of = _gla(proj, p["hgrn_lower_bounds"], nc, hc, c_dk, c_dv, l)
            xa = _gla(proj, p["hgrn_lower_bounds"], nc, hc, c_dk, c_dv, l, of=of, norm=p["hgrn_norm"][o])
            lam_init = 0.8 - 0.6 * math.exp(-0.3 * l)
            xb = _diff_attn(proj, 3 * hc * c_dk + 2 * hc * c_dv, t, hdh, d_hd, p["diff_q_norm"][o],
                            p["diff_k_norm"][o], p["diff_lambda"][o], p["diff_norm"][o], lam_init)
            w_out = p["cd_w_out"][o].astype(BF16)
        h = _matmul2_residual(xa, xb, w_out, h)
        hn = _rmsnorm_bf16(h, p["norm_ffn"][l])
        act = _ffn_up(hn, w_up, l, p["ffn_conv_w"][l], p["ffn_conv_b"][l], t)
        if l + 1 < depth:
            h = _matmul_residual(act, w_down, l, h)
    b = m // t
    s = t - PAD - N_META
    y_p = _matmul_residual_tokens(act, w_down, depth - 1, h, t, 0, bp, name="ffn_down_prompt")
    y_s = _matmul_residual_tokens(act, w_down, depth - 1, h, t, bp, b - bp, name="ffn_down_sample")
    return y_p.reshape(bp, s, d), y_s.reshape(b - bp, s, d)


def kernel(x_prompt, x_sample, meta_tokens, norm_mix, norm_ffn, ab_w_in, ab_w_out, mlstm_i_bias, mlstm_f_bias, mlstm_norm, dn_conv, dn_a_log, dn_dt_bias, dn_norm, cd_w_in, cd_w_out, hgrn_lower_bounds, hgrn_norm, diff_q_norm, diff_k_norm, diff_lambda, diff_norm, ffn_w_up, ffn_conv_w, ffn_conv_b, ffn_w_down):
    p = dict(norm_mix=norm_mix, norm_ffn=norm_ffn, ab_w_in=ab_w_in, ab_w_out=ab_w_out, mlstm_i_bias=mlstm_i_bias,
             mlstm_f_bias=mlstm_f_bias, mlstm_norm=mlstm_norm, dn_conv=dn_conv, dn_a_log=dn_a_log,
             dn_dt_bias=dn_dt_bias, dn_norm=dn_norm, cd_w_in=cd_w_in, cd_w_out=cd_w_out,
             hgrn_lower_bounds=hgrn_lower_bounds, hgrn_norm=hgrn_norm, diff_q_norm=diff_q_norm,
             diff_k_norm=diff_k_norm, diff_lambda=diff_lambda, diff_norm=diff_norm, ffn_w_up=ffn_w_up,
             ffn_conv_w=ffn_conv_w, ffn_conv_b=ffn_conv_b, ffn_w_down=ffn_w_down)
    bp, s, d = x_prompt.shape
    bs = x_sample.shape[0]
    assert x_sample.shape[1:] == (s, d)
    t = PAD + N_META + s
    assert t % CHUNK == 0
    b = bp + bs
    front = jnp.concatenate([jnp.zeros((PAD, d), F32), meta_tokens.astype(F32)], axis=0)
    h = jnp.concatenate(
        [jnp.concatenate([jnp.broadcast_to(front[None], (n, PAD + N_META, d)), x.astype(F32)], axis=1)
         for n, x in ((bp, x_prompt), (bs, x_sample))], axis=0).reshape(b * t, d)
    return _trunk(h, t, p, bp)
```
